```python
import math
import jax, jax.numpy as jnp
from jax import lax
import numpy as np

D_MODEL = 1024
BATCH = 4
SEQ = 8192
DEPTH = 2

N_A_LAYERS = DEPTH // 2
N_B_LAYERS = DEPTH - N_A_LAYERS
N_HEADS = 16
HEAD_DIM = 64
N_KV_GROUPS = 4
HEADS_PER_GROUP = N_HEADS // N_KV_GROUPS
D_FF = 4 * D_MODEL
CONV_WIDTH = 3
CMP_BLOCK = 32
CMP_STRIDE = 16
SEL_BLOCK = 64
SEL_TOP_N = 16
WINDOW = 512
PHI_HIDDEN = 256
N_BUCKETS = 32
MAX_DISTANCE = 128
Q_BLOCK = 128
EPS = 1e-6
NEG_INF = -1e30
FORCED_SCORE = 1e9

kernel_name = "yoco_shortconv_nsa_hybrid"


def rms_norm(x, g):
    xf = x.astype(jnp.float32)
    y = xf * lax.rsqrt(jnp.mean(xf * xf, axis=-1, keepdims=True) + EPS)
    return (y * g.astype(jnp.float32)).astype(x.dtype)


def t5_bucket(dist):
    dist = jnp.maximum(dist, 0)
    max_exact = N_BUCKETS // 2
    logd = jnp.log(jnp.maximum(dist, 1).astype(jnp.float32) / max_exact)
    large = max_exact + (logd / math.log(MAX_DISTANCE / max_exact)
                         * (N_BUCKETS - max_exact)).astype(jnp.int32)
    large = jnp.minimum(large, N_BUCKETS - 1)
    return jnp.where(dist < max_exact, dist, large)


def short_conv_mixer(h, w_in, conv_w, w_out):
    S = h.shape[1]
    b_gate, c_gate, u = jnp.split(h @ w_in, 3, axis=-1)
    v = c_gate * u
    vp = jnp.pad(v, ((0, 0), (CONV_WIDTH - 1, 0), (0, 0)))
    conv = sum(conv_w[k] * vp[:, k:k + S] for k in range(CONV_WIDTH))
    return (b_gate * conv) @ w_out


def squared_relu_mlp(h, w1, w2):
    return jnp.square(jax.nn.relu(h @ w1)) @ w2


def compress_blocks(t, pe, w1, w2):
    B, S, G, dh = t.shape
    chunks = t.reshape(B, S // CMP_STRIDE, CMP_STRIDE, G, dh)
    blocks = jnp.concatenate([chunks[:, :-1], chunks[:, 1:]], axis=2)
    blocks = blocks + pe[None, None, :, None, :]
    n_cmp = blocks.shape[1]
    flat = jnp.moveaxis(blocks, 2, 3).reshape(B, n_cmp, G, CMP_BLOCK * dh)
    return jax.nn.silu(flat @ w1) @ w2


def nsa_shared_kv(h, w_kv, cmp_pe_k, cmp_pe_v, phi_k_w1, phi_k_w2, phi_v_w1, phi_v_w2):
    B, S, _ = h.shape
    kv = (h @ w_kv).reshape(B, S, 6, N_KV_GROUPS, HEAD_DIM)
    k_raw, v_raw = kv[:, :, 0], kv[:, :, 1]
    k_sel, v_sel = kv[:, :, 2], kv[:, :, 3]
    k_win, v_win = kv[:, :, 4], kv[:, :, 5]
    k_cmp = compress_blocks(k_raw, cmp_pe_k, phi_k_w1, phi_k_w2)
    v_cmp = compress_blocks(v_raw, cmp_pe_v, phi_v_w1, phi_v_w2)
    return (k_cmp, v_cmp, k_sel, v_sel, k_win, v_win)


def nsa_attention(q, gates, k_cmp, v_cmp, k_sel, v_sel, k_win, v_win, rel_bias):
    B, S, G, Hg, dh = q.shape
    n_cmp = k_cmp.shape[1]
    n_selb = S // SEL_BLOCK
    top_n = min(SEL_TOP_N, n_selb)
    n_qb = S // Q_BLOCK
    bias_g = rel_bias.reshape(N_BUCKETS, G, Hg)

    cmp_end = jnp.arange(n_cmp) * CMP_STRIDE + CMP_BLOCK - 1
    ratio = SEL_BLOCK // CMP_STRIDE
    span = CMP_BLOCK // CMP_STRIDE
    offs = (jnp.arange(ratio)[:, None] - jnp.arange(span)[None, :]).reshape(-1)
    map_idx = ratio * jnp.arange(n_selb)[:, None] + offs[None, :]
    map_ok = (map_idx >= 0) & (map_idx < n_cmp)
    map_idx = jnp.clip(map_idx, 0, n_cmp - 1)

    k_sel_bg = jnp.transpose(k_sel.reshape(B, n_selb, SEL_BLOCK, G, dh), (0, 3, 1, 2, 4))
    v_sel_bg = jnp.transpose(v_sel.reshape(B, n_selb, SEL_BLOCK, G, dh), (0, 3, 1, 2, 4))
    k_win_p = jnp.pad(k_win, ((0, 0), (WINDOW, 0), (0, 0), (0, 0)))
    v_win_p = jnp.pad(v_win, ((0, 0), (WINDOW, 0), (0, 0), (0, 0)))
    b_ix = jnp.arange(B)[:, None, None, None]
    g_ix = jnp.arange(G)[None, None, :, None]
    jb = jnp.arange(n_selb)

    def one_block(qb):
        t0 = qb * Q_BLOCK
        qt = lax.dynamic_slice_in_dim(q, t0, Q_BLOCK, axis=1)
        gt = lax.dynamic_slice_in_dim(gates, t0, Q_BLOCK, axis=1)
        pos = t0 + jnp.arange(Q_BLOCK)

        d_c = pos[:, None] - cmp_end[None, :]
        valid_c = d_c >= 0
        bias_c = jnp.transpose(bias_g[t5_bucket(d_c)], (0, 2, 3, 1)).astype(jnp.float32)
        logit_c = jnp.einsum('bqghd,bngd->bqghn', qt, k_cmp).astype(jnp.float32) + bias_c[None]
        logit_c = jnp.where(valid_c[None, :, None, None, :], logit_c, NEG_INF)
        has_any = jnp.any(valid_c, axis=-1).astype(jnp.float32)
        p_c = jax.nn.softmax(logit_c, axis=-1) * has_any[None, :, None, None, None]
        o_c = jnp.einsum('bqghn,bngd->bqghd', p_c.astype(v_cmp.dtype), v_cmp)

        imp = p_c.sum(axis=3)
        imp_s = jnp.sum(jnp.where(map_ok, jnp.take(imp, map_idx, axis=-1), 0.0), axis=-1)
        cb = pos // SEL_BLOCK
        forced = (jb[None, :] == 0) | (jb[None, :] == cb[:, None]) | (jb[None, :] == cb[:, None] - 1)
        future = jb[None, :] > cb[:, None]
        score = jnp.where(forced[None, :, None, :], FORCED_SCORE, imp_s)
        score = jnp.where(future[None, :, None, :], NEG_INF, score)
        top_s, top_idx = lax.top_k(score, top_n)
        blk_ok = top_s > NEG_INF * 0.5

        ks = k_sel_bg[b_ix, g_ix, top_idx].reshape(B, Q_BLOCK, G, top_n * SEL_BLOCK, dh)
        vs = v_sel_bg[b_ix, g_ix, top_idx].reshape(B, Q_BLOCK, G, top_n * SEL_BLOCK, dh)
        kpos = (top_idx[..., None] * SEL_BLOCK + jnp.arange(SEL_BLOCK)).reshape(B, Q_BLOCK, G, -1)
        blk_ok_k = jnp.repeat(blk_ok, SEL_BLOCK, axis=-1)
        d_s = pos[None, :, None, None] - kpos
        valid_s = blk_ok_k & (d_s >= 0)
        bias_s = jnp.transpose(bias_g[t5_bucket(d_s), g_ix], (0, 1, 2, 4, 3)).astype(jnp.float32)
        logit_s = jnp.einsum('bqghd,bqgkd->bqghk', qt, ks).astype(jnp.float32) + bias_s
        logit_s = jnp.where(valid_s[:, :, :, None, :], logit_s, NEG_INF)
        p_s = jax.nn.softmax(logit_s, axis=-1)
        o_s = jnp.einsum('bqghk,bqgkd->bqghd', p_s.astype(vs.dtype), vs)

        kw = lax.dynamic_slice_in_dim(k_win_p, t0, Q_BLOCK + WINDOW, axis=1)
        vw = lax.dynamic_slice_in_dim(v_win_p, t0, Q_BLOCK + WINDOW, axis=1)
        wpos = t0 - WINDOW + jnp.arange(Q_BLOCK + WINDOW)
        d_w = pos[:, None] - wpos[None, :]
        valid_w = (d_w >= 0) & (d_w < WINDOW) & (wpos[None, :] >= 0)
        bias_w = jnp.transpose(bias_g[t5_bucket(d_w)], (0, 2, 3, 1)).astype(jnp.float32)
        logit_w = jnp.einsum('bqghd,bkgd->bqghk', qt, kw).astype(jnp.float32) + bias_w[None]
        logit_w = jnp.where(valid_w[None, :, None, None, :], logit_w, NEG_INF)
        p_w = jax.nn.softmax(logit_w, axis=-1)
        o_w = jnp.einsum('bqghk,bkgd->bqghd', p_w.astype(vw.dtype), vw)

        o = gt[..., 0:1] * o_c + gt[..., 1:2] * o_s + gt[..., 2:3] * o_w
        return o.astype(q.dtype)

    o = lax.map(one_block, jnp.arange(n_qb))
    return jnp.moveaxis(o, 0, 1).reshape(B, S, G, Hg, dh)


def nsa_mixer(h, w_qg, w_o, shared, rel_bias):
    B, S, _ = h.shape
    qg = h @ w_qg
    q = qg[..., :N_HEADS * HEAD_DIM].reshape(B, S, N_KV_GROUPS, HEADS_PER_GROUP, HEAD_DIM)
    q = q * (HEAD_DIM ** -0.5)
    gates = jax.nn.sigmoid(qg[..., N_HEADS * HEAD_DIM:].astype(jnp.float32))
    gates = gates.reshape(B, S, N_KV_GROUPS, HEADS_PER_GROUP, 3)
    k_cmp, v_cmp, k_sel, v_sel, k_win, v_win = shared
    o = nsa_attention(q, gates, k_cmp, v_cmp, k_sel, v_sel, k_win, v_win, rel_bias)
    return o.reshape(B, S, N_HEADS * HEAD_DIM) @ w_o


def setup_inputs(seed: int = 0) -> dict:
    key = jax.random.key(seed)
    ks = jax.random.split(key, 22)
    f32 = jnp.float32
    D, H, G, dh = D_MODEL, N_HEADS, N_KV_GROUPS, HEAD_DIM

    def nrm(k, shape, scale):
        return jax.random.normal(k, shape, f32) * scale

    return {
        "x": nrm(ks[0], (BATCH, SEQ, D), 1.0),
        "rel_bias": nrm(ks[1], (N_BUCKETS, H), 0.5),
        "norm_mix": 1.0 + nrm(ks[2], (DEPTH, D), 0.1),
        "norm_mlp": 1.0 + nrm(ks[3], (DEPTH, D), 0.1),
        "mlp_w1": nrm(ks[4], (DEPTH, D, D_FF), D ** -0.5),
        "mlp_w2": nrm(ks[5], (DEPTH, D_FF, D), 0.5 * D_FF ** -0.5),
        "a_w_in": nrm(ks[6], (N_A_LAYERS, D, 3 * D), D ** -0.5),
        "a_conv_w": nrm(ks[7], (N_A_LAYERS, CONV_WIDTH, D), CONV_WIDTH ** -0.5),
        "a_w_out": nrm(ks[8], (N_A_LAYERS, D, D), D ** -0.5),
        "kv_norm": 1.0 + nrm(ks[9], (D,), 0.1),
        "w_kv": nrm(ks[10], (D, 6 * G * dh), D ** -0.5),
        "cmp_pe_k": nrm(ks[11], (CMP_BLOCK, dh), 0.1),
        "cmp_pe_v": nrm(ks[12], (CMP_BLOCK, dh), 0.1),
        "phi_k_w1": nrm(ks[13], (CMP_BLOCK * dh, PHI_HIDDEN), (CMP_BLOCK * dh) ** -0.5),
        "phi_k_w2": nrm(ks[14], (PHI_HIDDEN, dh), PHI_HIDDEN ** -0.5),
        "phi_v_w1": nrm(ks[15], (CMP_BLOCK * dh, PHI_HIDDEN), (CMP_BLOCK * dh) ** -0.5),
        "phi_v_w2": nrm(ks[16], (PHI_HIDDEN, dh), PHI_HIDDEN ** -0.5),
        "b_w_qg": nrm(ks[17], (N_B_LAYERS, D, H * dh + 3 * H), D ** -0.5),
        "b_w_o": nrm(ks[18], (N_B_LAYERS, H * dh, D), (H * dh) ** -0.5),
        "final_norm": 1.0 + nrm(ks[19], (D,), 0.1),
    }


def reference(x, rel_bias, norm_mix, norm_mlp, mlp_w1, mlp_w2, a_w_in, a_conv_w, a_w_out,
              kv_norm, w_kv, cmp_pe_k, cmp_pe_v, phi_k_w1, phi_k_w2, phi_v_w1, phi_v_w2,
              b_w_qg, b_w_o, final_norm):
    shared = None
    for layer in range(DEPTH):
        h = rms_norm(x, norm_mix[layer])
        if layer < N_A_LAYERS:
            x = x + short_conv_mixer(h, a_w_in[layer], a_conv_w[layer], a_w_out[layer])
        else:
            i = layer - N_A_LAYERS
            x = x + nsa_mixer(h, b_w_qg[i], b_w_o[i], shared, rel_bias)
        x = x + squared_relu_mlp(rms_norm(x, norm_mlp[layer]), mlp_w1[layer], mlp_w2[layer])
        if layer == N_A_LAYERS - 1:
            shared = nsa_shared_kv(rms_norm(x, kv_norm), w_kv, cmp_pe_k, cmp_pe_v,
                                   phi_k_w1, phi_k_w2, phi_v_w1, phi_v_w2)
    return rms_norm(x, final_norm)
```

```python
import functools
import math

import numpy as np
import jax
import jax.numpy as jnp
from jax import lax
from jax.experimental import pallas as pl
from jax.experimental.pallas import tpu as pltpu

D_MODEL = 1024
N_HEADS = 16
HEAD_DIM = 64
N_GROUPS = 4
HPG = N_HEADS // N_GROUPS
D_FF = 4 * D_MODEL
CONV_WIDTH = 3
CMP_BLOCK = 32
CMP_STRIDE = 16
SEL_BLOCK = 64
SEL_TOP_N = 16
WINDOW = 512
PHI_HIDDEN = 256
N_BUCKETS = 32
MAX_DISTANCE = 128
EPS = 1e-6
NEG_INF = -1e30
FORCED_SCORE = 1e9
REMOVED_SCORE = -3e38

QT = 128
QL = HPG * QT
KA = 128
VA = 80
MACRO = 16
MACRO_KEYS = MACRO * SEL_BLOCK
CMP_TAB = 24
VMEM_LIMIT = 56 * 1024 * 1024

F32 = jnp.float32
BF16 = jnp.bfloat16


def _dot(a, b):
    return jnp.dot(a, b, preferred_element_type=F32)


def _dot_nt(a, b):
    return lax.dot_general(a, b, (((1,), (1,)), ((), ())), preferred_element_type=F32)


def _rms(x, g):
    ms = jnp.mean(x * x, axis=-1, keepdims=True)
    return x * lax.rsqrt(ms + EPS) * g


def _resident(shape):
    zeros = (0,) * len(shape)
    return pl.BlockSpec(shape, lambda *_: zeros, pipeline_mode=pl.Buffered(1))


def _params(*sem):
    return pltpu.CompilerParams(dimension_semantics=sem, vmem_limit_bytes=VMEM_LIMIT)


CONV_CHUNK = 512


def _mix0_kernel(x_ref, g_ref, win_ref, cw_ref, wout_ref, o_ref, carry_ref, *, tm):
    @pl.when(pl.program_id(1) == 0)
    def _():
        carry_ref[...] = jnp.zeros_like(carry_ref)

    x = x_ref[0]
    h = _rms(x, g_ref[...]).astype(BF16)
    row = lax.broadcasted_iota(jnp.int32, (tm, CONV_CHUNK), 0)
    acc = jnp.zeros((tm, D_MODEL), F32)
    for c in range(D_MODEL // CONV_CHUNK):
        lo, hi = c * CONV_CHUNK, (c + 1) * CONV_CHUNK
        b_gate = _dot(h, win_ref[:, lo:hi])
        c_gate = _dot(h, win_ref[:, D_MODEL + lo:D_MODEL + hi])
        u = _dot(h, win_ref[:, 2 * D_MODEL + lo:2 * D_MODEL + hi])
        v = c_gate * u
        prev = carry_ref[:, lo:hi]
        v1 = jnp.where(row == 0, prev[7:8], pltpu.roll(v, 1, axis=0))
        v2 = pltpu.roll(v, 2, axis=0)
        v2 = jnp.where(row == 0, prev[6:7], jnp.where(row == 1, prev[7:8], v2))
        carry_ref[:, lo:hi] = v[tm - 8:tm]
        conv = cw_ref[0:1, lo:hi] * v2 + cw_ref[1:2, lo:hi] * v1 + cw_ref[2:3, lo:hi] * v
        acc = acc + _dot((b_gate * conv).astype(BF16), wout_ref[lo:hi, :])
    o_ref[0] = x + acc


def _mix0(x, g, w_in, conv_w, w_out, tm):
    B, S, D = x.shape
    return pl.pallas_call(
        functools.partial(_mix0_kernel, tm=tm),
        out_shape=jax.ShapeDtypeStruct((B, S, D), F32),
        grid=(B, S // tm),
        in_specs=[
            pl.BlockSpec((1, tm, D), lambda b, j: (b, j, 0)),
            _resident((1, D)),
            _resident((D, 3 * D)),
            _resident((CONV_WIDTH, D)),
            _resident((D, D)),
        ],
        out_specs=pl.BlockSpec((1, tm, D), lambda b, j: (b, j, 0)),
        scratch_shapes=[pltpu.VMEM((8, D), F32)],
        compiler_params=_params("arbitrary", "arbitrary"),
        name="mix0",
    )(x, g, w_in, conv_w, w_out)


FF_CHUNK = 1024


def _mlp_kernel(*refs, has_proj, has_final):
    refs = list(refs)
    x_ref = refs.pop(0)
    if has_proj:
        a_ref, wo_ref = refs.pop(0), refs.pop(0)
    g_ref, w1_ref, w2_ref = refs.pop(0), refs.pop(0), refs.pop(0)
    if has_final:
        gf_ref = refs.pop(0)
    (o_ref,) = refs

    x = x_ref[...]
    if has_proj:
        x = x + _dot(a_ref[...], wo_ref[...])
    h = _rms(x, g_ref[...]).astype(BF16)
    acc = jnp.zeros_like(x)
    for c in range(D_FF // FF_CHUNK):
        lo, hi = c * FF_CHUNK, (c + 1) * FF_CHUNK
        a = jnp.maximum(_dot(h, w1_ref[:, lo:hi]), 0.0)
        acc = acc + _dot((a * a).astype(BF16), w2_ref[lo:hi, :])
    y = x + acc
    if has_final:
        y = _rms(y, gf_ref[...])
    o_ref[...] = y


def _mlp(x, g, w1, w2, tm, attn=None, w_o=None, g_final=None):
    T, D = x.shape
    has_proj, has_final = attn is not None, g_final is not None
    row = pl.BlockSpec((tm, D), lambda i: (i, 0))
    args, specs = [x], [row]
    if has_proj:
        args += [attn, w_o]
        specs += [row, _resident((D, D))]
    args += [g, w1, w2]
    specs += [_resident((1, D)), _resident((D, D_FF)), _resident((D_FF, D))]
    if has_final:
        args.append(g_final)
        specs.append(_resident((1, D)))
    return pl.pallas_call(
        functools.partial(_mlp_kernel, has_proj=has_proj, has_final=has_final),
        out_shape=jax.ShapeDtypeStruct((T, D), F32),
        grid=(T // tm,),
        in_specs=specs,
        out_specs=row,
        compiler_params=_params("arbitrary"),
        name="mlp_out" if has_proj else "mlp",
    )(*args)


TOK_COLS = 2 * N_GROUPS * HEAD_DIM + 2 * N_GROUPS * KA
FEAT_ROWS = 2 * N_GROUPS * VA


def _kvproj_kernel(x_ref, g_ref, wtok_ref, wfeat_ref,
                   kraw_ref, vraw_ref, ksel_ref, kwin_ref, vsel_ref, vwin_ref, *, tm):
    gd = N_GROUPS * HEAD_DIM
    h = _rms(x_ref[0], g_ref[...]).astype(BF16)
    tok = _dot(h, wtok_ref[...])
    kraw_ref[0] = tok[:, 0:gd].astype(BF16)
    vraw_ref[0] = tok[:, gd:2 * gd].astype(BF16)
    pos = pl.program_id(1) * tm + lax.broadcasted_iota(jnp.int32, (tm, KA), 0)
    lane = lax.broadcasted_iota(jnp.int32, (tm, KA), 1)
    in_macro = jnp.right_shift(jnp.bitwise_and(pos, MACRO_KEYS - 1), int(math.log2(SEL_BLOCK)))
    onehot = (lane == HEAD_DIM + in_macro).astype(F32)
    for g in range(N_GROUPS):
        base = 2 * gd + g * KA
        ksel_ref[0, g] = (tok[:, base:base + KA] + onehot).astype(BF16)
        base = 2 * gd + N_GROUPS * KA + g * KA
        kwin_ref[0, g] = tok[:, base:base + KA].astype(BF16)
    feat = _dot_nt(wfeat_ref[...], h)
    ones_row = (lax.broadcasted_iota(jnp.int32, (VA, tm), 0) == HEAD_DIM).astype(F32)
    for g in range(N_GROUPS):
        vsel_ref[0, g] = (feat[g * VA:(g + 1) * VA] + ones_row).astype(BF16)
        base = N_GROUPS * VA + g * VA
        vwin_ref[0, g] = (feat[base:base + VA] + ones_row).astype(BF16)


def _kvproj(x, g, w_tok, w_feat, tm):
    B, S, D = x.shape
    gd = N_GROUPS * HEAD_DIM
    tokmaj = lambda w: pl.BlockSpec((1, tm, w), lambda b, j: (b, j, 0))
    return pl.pallas_call(
        functools.partial(_kvproj_kernel, tm=tm),
        out_shape=[
            jax.ShapeDtypeStruct((B, S, gd), BF16),
            jax.ShapeDtypeStruct((B, S, gd), BF16),
            jax.ShapeDtypeStruct((B, N_GROUPS, S, KA), BF16),
            jax.ShapeDtypeStruct((B, N_GROUPS, S, KA), BF16),
            jax.ShapeDtypeStruct((B, N_GROUPS, VA, S), BF16),
            jax.ShapeDtypeStruct((B, N_GROUPS, VA, S), BF16),
        ],
        grid=(B, S // tm),
        in_specs=[
            pl.BlockSpec((1, tm, D), lambda b, j: (b, j, 0)),
            _resident((1, D)),
            _resident((D, TOK_COLS)),
            _resident((FEAT_ROWS, D)),
        ],
        out_specs=[
            tokmaj(gd), tokmaj(gd),
            pl.BlockSpec((1, N_GROUPS, tm, KA), lambda b, j: (b, 0, j, 0)),
            pl.BlockSpec((1, N_GROUPS, tm, KA), lambda b, j: (b, 0, j, 0)),
            pl.BlockSpec((1, N_GROUPS, VA, tm), lambda b, j: (b, 0, 0, j)),
            pl.BlockSpec((1, N_GROUPS, VA, tm), lambda b, j: (b, 0, 0, j)),
        ],
        compiler_params=_params("arbitrary", "arbitrary"),
        name="kvproj",
    )(x, g, w_tok, w_feat)


def _compress_kernel(tk_ref, tv_ref, w1k_ref, w1v_ref, pek_ref, pev_ref, w2k_ref, w2vt_ref,
                     kc_ref, vct_ref, *, nc):
    half = CMP_STRIDE * HEAD_DIM

    def hidden(t_ref, w1_ref, pe_ref):
        ab = _dot(t_ref[0, 0], w1_ref[...])
        pe = (_dot(pe_ref[:, 0:half], w1_ref[:, 0:PHI_HIDDEN])
              + _dot(pe_ref[:, half:2 * half], w1_ref[:, PHI_HIDDEN:2 * PHI_HIDDEN]))
        nxt = pltpu.roll(ab[:, PHI_HIDDEN:2 * PHI_HIDDEN], nc - 1, axis=0)
        z = ab[:, 0:PHI_HIDDEN] + nxt + pe[0:1]
        return (z * jax.nn.sigmoid(z)).astype(BF16)

    kc = _dot(hidden(tk_ref, w1k_ref, pek_ref), w2k_ref[...])
    real = lax.broadcasted_iota(jnp.int32, (nc, KA), 0) < nc - 1
    kc_ref[0, 0] = jnp.where(real, kc, 0.0).astype(BF16)
    vct = _dot_nt(w2vt_ref[...], hidden(tv_ref, w1v_ref, pev_ref))
    real = lax.broadcasted_iota(jnp.int32, (VA, nc), 1) < nc - 1
    ones_row = (lax.broadcasted_iota(jnp.int32, (VA, nc), 0) == HEAD_DIM).astype(F32)
    vct_ref[0, 0] = (jnp.where(real, vct, 0.0) + ones_row).astype(BF16)


def _compress(tk, tv, w1k, w1v, pek, pev, w2k, w2vt):
    B, G, nc, width = tk.shape
    blk = pl.BlockSpec((1, 1, nc, width), lambda b, g: (b, g, 0, 0))
    return pl.pallas_call(
        functools.partial(_compress_kernel, nc=nc),
        out_shape=[
            jax.ShapeDtypeStruct((B, G, nc, KA), BF16),
            jax.ShapeDtypeStruct((B, G, VA, nc), BF16),
        ],
        grid=(B, G),
        in_specs=[
            blk, blk,
            _resident(w1k.shape), _resident(w1v.shape),
            _resident(pek.shape), _resident(pev.shape),
            _resident(w2k.shape), _resident(w2vt.shape),
        ],
        out_specs=[
            pl.BlockSpec((1, 1, nc, KA), lambda b, g: (b, g, 0, 0)),
            pl.BlockSpec((1, 1, VA, nc), lambda b, g: (b, g, 0, 0)),
        ],
        compiler_params=_params("arbitrary", "arbitrary"),
        name="compress",
    )(tk, tv, w1k, w1v, pek, pev, w2k, w2vt)


GATE_ROWS = 16


def _qproj_kernel(x_ref, g_ref, wq_ref, wg_ref, q_ref, gate_ref):
    h = _rms(x_ref[0], g_ref[...]).astype(BF16)
    q_ref[0] = (_dot_nt(wq_ref[...], h) * (HEAD_DIM ** -0.5)).astype(BF16)
    gate_ref[0] = jax.nn.sigmoid(_dot_nt(wg_ref[...], h))


def _qproj(x, g, wq_t, wg_t, tm):
    B, S, D = x.shape
    hd = N_HEADS * HEAD_DIM
    gr = N_GROUPS * GATE_ROWS
    return pl.pallas_call(
        _qproj_kernel,
        out_shape=[
            jax.ShapeDtypeStruct((B, hd, S), BF16),
            jax.ShapeDtypeStruct((B, gr, S), F32),
        ],
        grid=(B, S // tm),
        in_specs=[
            pl.BlockSpec((1, tm, D), lambda b, j: (b, j, 0)),
            _resident((1, D)),
            _resident((hd, D)),
            _resident((gr, D)),
        ],
        out_specs=[
            pl.BlockSpec((1, hd, tm), lambda b, j: (b, 0, j)),
            pl.BlockSpec((1, gr, tm), lambda b, j: (b, 0, j)),
        ],
        compiler_params=_params("arbitrary", "arbitrary"),
        name="qproj",
    )(x, g, wq_t, wg_t)


def _t5_bucket(dist):
    max_exact = N_BUCKETS // 2
    logd = jnp.log(jnp.maximum(dist, 1).astype(F32) / max_exact)
    large = max_exact + (logd / math.log(MAX_DISTANCE / max_exact)
                         * (N_BUCKETS - max_exact)).astype(jnp.int32)
    large = jnp.minimum(large, N_BUCKETS - 1)
    return jnp.where(dist < 0, -1, jnp.where(dist < max_exact, dist, large))


def _bucket_tables():
    key = jnp.arange(QT, dtype=jnp.int32)[:, None]
    qry = jnp.arange(QT, dtype=jnp.int32)[None, :]
    diag = _t5_bucket(qry - key)
    prev = _t5_bucket(qry - key + QT)
    blk = jnp.arange(CMP_TAB, dtype=jnp.int32)[:, None] - (CMP_TAB - QT // CMP_STRIDE)
    cmp_ = _t5_bucket(qry - CMP_STRIDE * blk - (CMP_BLOCK - 1))
    return diag, prev, cmp_


def _bias_kernel(rb_ref, diag_ref, prev_ref, cmp_ref, tdiag_ref, tprev_ref, tcmp_ref):
    g = pl.program_id(0)

    def table(bucket_ref, out_ref):
        bucket = bucket_ref[...]
        for hg in range(HPG):
            head = g * HPG + hg
            t = jnp.zeros(bucket.shape, F32)
            for b in range(N_BUCKETS):
                t = jnp.where(bucket == b, rb_ref[b, head], t)
            t = jnp.where(bucket >= 0, t - rb_ref[N_BUCKETS - 1, head], NEG_INF)
            out_ref[0, :, hg * QT:(hg + 1) * QT] = t

    table(diag_ref, tdiag_ref)
    table(prev_ref, tprev_ref)
    table(cmp_ref, tcmp_ref)


def _bias_tables(rel_bias):
    diag, prev, cmp_ = _bucket_tables()
    whole = lambda a: pl.BlockSpec(a.shape, lambda g: (0, 0))
    return pl.pallas_call(
        _bias_kernel,
        out_shape=[
            jax.ShapeDtypeStruct((N_GROUPS, QT, QL), F32),
            jax.ShapeDtypeStruct((N_GROUPS, QT, QL), F32),
            jax.ShapeDtypeStruct((N_GROUPS, CMP_TAB, QL), F32),
        ],
        grid=(N_GROUPS,),
        in_specs=[
            pl.BlockSpec(memory_space=pltpu.SMEM),
            whole(diag), whole(prev), whole(cmp_),
        ],
        out_specs=[
            pl.BlockSpec((1, QT, QL), lambda g: (g, 0, 0)),
            pl.BlockSpec((1, QT, QL), lambda g: (g, 0, 0)),
            pl.BlockSpec((1, CMP_TAB, QL), lambda g: (g, 0, 0)),
        ],
        compiler_params=_params("arbitrary"),
        name="bias_tables",
    )(rel_bias, diag, prev, cmp_)


SC_PAD = 16
IMP_PAD = 8
FAR_KEYS = 512


def _attn_kernel(q_ref, gate_ref, kc_ref, vct_ref, ks_ref, vst_ref, kw_ref, vwt_ref,
                 tdiag_ref, tprev_ref, twin_ref, tcmp_ref, o_ref,
                 w_ref, sc_ref, imp_ref, mask_ref, accs_ref, ms_ref, accw_ref, mw_ref, *, nc):
    qt = pl.program_id(2)
    t0 = pl.multiple_of(qt * QT, QT)
    nsel = nc // (SEL_BLOCK // CMP_STRIDE)

    for hg in range(HPG):
        w_ref[0:HEAD_DIM, hg * QT:(hg + 1) * QT] = q_ref[0, hg * HEAD_DIM:(hg + 1) * HEAD_DIM, :]
    w_ref[HEAD_DIM:KA, :] = jnp.zeros((KA - HEAD_DIM, QL), BF16)

    sc_ref[0:SC_PAD, :] = jnp.zeros((SC_PAD, QL), F32)
    sc_ref[SC_PAD:SC_PAD + nc, :] = _dot(kc_ref[0, 0], w_ref[...])
    win = pl.ds(pl.multiple_of(qt * (QT // CMP_STRIDE), 8), CMP_TAB)
    sc_ref[win, :] = sc_ref[win, :] + tcmp_ref[0]
    s = sc_ref[SC_PAD:SC_PAD + nc, :]
    blk = lax.broadcasted_iota(jnp.int32, (nc, QL), 0)
    s = jnp.where(blk < (qt + 1) * (QT // CMP_STRIDE), s, NEG_INF)
    m = jnp.max(s, axis=0, keepdims=True)
    e = jnp.exp(s - m)
    has_any = m > 0.5 * NEG_INF
    p = e * jnp.where(has_any, 1.0 / jnp.sum(e, axis=0, keepdims=True), 0.0)
    o_cmp = _dot(vct_ref[0, 0], p.astype(BF16))

    imp_ref[0:IMP_PAD, :] = jnp.zeros((IMP_PAD, QT), F32)
    imp_ref[IMP_PAD:IMP_PAD + nc, :] = (p[:, 0:QT] + p[:, QT:2 * QT]
                                        + p[:, 2 * QT:3 * QT] + p[:, 3 * QT:4 * QT])
    imp_ref[IMP_PAD + nc:IMP_PAD + nc + 8, :] = jnp.zeros((8, QT), F32)
    ratio = SEL_BLOCK // CMP_STRIDE
    part = lambda r: imp_ref[pl.ds(IMP_PAD + r, nsel, stride=ratio), :]
    imp_s = part(-1) + 2.0 * (part(0) + part(1) + part(2)) + part(3)

    jb = lax.broadcasted_iota(jnp.int32, (nsel, QT), 0)
    cb = 2 * qt + (lax.broadcasted_iota(jnp.int32, (nsel, QT), 1) >= SEL_BLOCK).astype(jnp.int32)
    forced = (jb == 0) | (jb == cb) | (jb == cb - 1)
    score = jnp.where(forced, FORCED_SCORE, imp_s)
    score = jnp.where(jb > cb, NEG_INF, score)

    jbf = jb.astype(F32)

    def take_one(_, carry):
        score, sel = carry
        top = jnp.max(score, axis=0, keepdims=True)
        first = jnp.min(jnp.where(score == top, jbf, float(nsel)), axis=0, keepdims=True)
        pick = jbf == first
        sel = jnp.where(pick & (top > 0.5 * NEG_INF), 1.0, sel)
        return jnp.where(pick, REMOVED_SCORE, score), sel

    _, sel = lax.fori_loop(0, SEL_TOP_N, take_one, (score, jnp.zeros((nsel, QT), F32)), unroll=True)
    mask_ref[...] = jnp.where(sel > 0.0, 0.0, NEG_INF).astype(BF16)

    def attend(k, vt, bias, acc_ref, m_ref):
        s = _dot(k, w_ref[...])
        if bias is not None:
            s = s + bias
        m_old = m_ref[0:1, :]
        m_new = jnp.maximum(m_old, jnp.max(s, axis=0, keepdims=True))
        p = jnp.exp(s - m_new).astype(BF16)
        acc_ref[...] = jnp.exp(m_old - m_new) * acc_ref[...] + _dot(vt, p)
        m_ref[0:1, :] = m_new

    def reset(acc_ref, m_ref):
        acc_ref[...] = jnp.zeros_like(acc_ref)
        m_ref[...] = jnp.full(m_ref.shape, NEG_INF, F32)

    reset(accw_ref, mw_ref)

    def win_chunk(back, bias):
        start = pl.multiple_of(t0 - back * QT, QT)
        attend(kw_ref[0, 0, pl.ds(start, QT), :], vwt_ref[0, 0, :, pl.ds(start, QT)],
               bias, accw_ref, mw_ref)

    win_chunk(0, tdiag_ref[0])
    for back, bias_of in ((1, lambda: tprev_ref[0]), (2, None), (3, None), (4, lambda: twin_ref[...])):
        @pl.when(qt >= back)
        def _(back=back, bias_of=bias_of):
            win_chunk(back, None if bias_of is None else bias_of())

    reset(accs_ref, ms_ref)

    def set_mask_rows(u):
        rows = mask_ref[pl.ds(pl.multiple_of(u * MACRO, MACRO), MACRO), :]
        for hg in range(HPG):
            w_ref[HEAD_DIM:HEAD_DIM + MACRO, hg * QT:(hg + 1) * QT] = rows

    def sel_chunk(start, n, bias):
        attend(ks_ref[0, 0, pl.ds(start, n), :], vst_ref[0, 0, :, pl.ds(start, n)],
               bias, accs_ref, ms_ref)

    set_mask_rows(qt // (MACRO_KEYS // QT))
    sel_chunk(t0, QT, tdiag_ref[0])

    @pl.when(qt >= 1)
    def _():
        full = (qt - 1) // (MACRO_KEYS // QT)
        rest = (qt - 1) % (MACRO_KEYS // QT)
        set_mask_rows(full)
        sel_chunk(pl.multiple_of(t0 - QT, QT), QT, tprev_ref[0])

        def rest_chunk(r, _):
            sel_chunk(pl.multiple_of(full * MACRO_KEYS + r * QT, QT), QT, None)
            return 0

        lax.fori_loop(0, rest, rest_chunk, 0)

        def macro_tile(u, _):
            set_mask_rows(u)
            for part_ in range(MACRO_KEYS // FAR_KEYS):
                sel_chunk(pl.multiple_of(u * MACRO_KEYS + part_ * FAR_KEYS, FAR_KEYS), FAR_KEYS, None)
            return 0

        lax.fori_loop(0, full, macro_tile, 0)

    o_sel = accs_ref[0:HEAD_DIM, :] * (1.0 / accs_ref[HEAD_DIM:HEAD_DIM + 1, :])
    o_win = accw_ref[0:HEAD_DIM, :] * (1.0 / accw_ref[HEAD_DIM:HEAD_DIM + 1, :])
    heads = []
    for hg in range(HPG):
        cols = slice(hg * QT, (hg + 1) * QT)
        gate = lambda br: gate_ref[0, hg * 3 + br:hg * 3 + br + 1, :]
        heads.append(gate(0) * o_cmp[0:HEAD_DIM, cols] + gate(1) * o_sel[:, cols] + gate(2) * o_win[:, cols])
    for pair in range(HPG // 2):
        both = jnp.concatenate(heads[2 * pair:2 * pair + 2], axis=0)
        o_ref[0, :, pair * 2 * HEAD_DIM:(pair + 1) * 2 * HEAD_DIM] = both.T.astype(BF16)


def _attention(q_t, gates_t, kc, vct, ks, vst, kw, vwt, tdiag, tprev, twin, tcmp):
    B, hd, S = q_t.shape
    nc = kc.shape[2]
    per_bg = lambda *blk: pl.BlockSpec((1, 1) + blk, lambda b, g, j: (b, g, 0, 0))
    per_g = lambda rows: pl.BlockSpec((1, rows, QL), lambda b, g, j: (g, 0, 0))
    return pl.pallas_call(
        functools.partial(_attn_kernel, nc=nc),
        out_shape=jax.ShapeDtypeStruct((B, S, hd), BF16),
        grid=(B, N_GROUPS, S // QT),
        in_specs=[
            pl.BlockSpec((1, HPG * HEAD_DIM, QT), lambda b, g, j: (b, g, j)),
            pl.BlockSpec((1, GATE_ROWS, QT), lambda b, g, j: (b, g, j)),
            per_bg(nc, KA), per_bg(VA, nc),
            per_bg(S, KA), per_bg(VA, S),
            per_bg(S, KA), per_bg(VA, S),
            per_g(QT), per_g(QT),
            pl.BlockSpec((QT, QL), lambda b, g, j: (0, 0)),
            per_g(CMP_TAB),
        ],
        out_specs=pl.BlockSpec((1, QT, HPG * HEAD_DIM), lambda b, g, j: (b, j, g)),
        scratch_shapes=[
            pltpu.VMEM((KA, QL), BF16),
            pltpu.VMEM((SC_PAD + nc, QL), F32),
            pltpu.VMEM((IMP_PAD + nc + 8, QT), F32),
            pltpu.VMEM((nc // (SEL_BLOCK // CMP_STRIDE), QT), BF16),
            pltpu.VMEM((VA, QL), F32), pltpu.VMEM((8, QL), F32),
            pltpu.VMEM((VA, QL), F32), pltpu.VMEM((8, QL), F32),
        ],
        compiler_params=_params("arbitrary", "arbitrary", "arbitrary"),
        name="nsa_attention",
    )(q_t, gates_t, kc, vct, ks, vst, kw, vwt, tdiag, tprev, twin, tcmp)


def _pad_groups(w, width):
    d = w.shape[0]
    w = w.reshape(d, N_GROUPS, HEAD_DIM)
    return jnp.pad(w, ((0, 0), (0, 0), (0, width - HEAD_DIM))).reshape(d, N_GROUPS * width)


def kernel(x, rel_bias, norm_mix, norm_mlp, mlp_w1, mlp_w2, a_w_in, a_conv_w, a_w_out, kv_norm, w_kv,
           cmp_pe_k, cmp_pe_v, phi_k_w1, phi_k_w2, phi_v_w1, phi_v_w2, b_w_qg, b_w_o, final_norm):
    B, S, D = x.shape
    assert D == D_MODEL and S % MACRO_KEYS == 0 and norm_mix.shape[0] == 2
    tm = min(512, S)
    nc = S // CMP_STRIDE
    gd = N_GROUPS * HEAD_DIM
    row = lambda v: v.reshape(1, D).astype(F32)

    x = _mix0(x, row(norm_mix[0]), a_w_in[0].astype(BF16), a_conv_w[0], a_w_out[0].astype(BF16), tm)
    x = _mlp(x.reshape(B * S, D), row(norm_mlp[0]), mlp_w1[0].astype(BF16), mlp_w2[0].astype(BF16), tm)
    x = x.reshape(B, S, D)

    part = lambda c: w_kv[:, c * gd:(c + 1) * gd]
    w_tok = jnp.concatenate([part(0), part(1), _pad_groups(part(2), KA), _pad_groups(part(4), KA)], axis=1)
    w_feat = jnp.concatenate([_pad_groups(part(3), VA).T, _pad_groups(part(5), VA).T], axis=0)
    k_raw, v_raw, k_sel, k_win, v_sel_t, v_win_t = _kvproj(
        x, row(kv_norm), w_tok.astype(BF16), w_feat.astype(BF16), tm)

    def chunks(t):
        t = t.reshape(B, nc, CMP_STRIDE, N_GROUPS, HEAD_DIM)
        return jnp.transpose(t, (0, 3, 1, 2, 4)).reshape(B, N_GROUPS, nc, CMP_STRIDE * HEAD_DIM)

    half = CMP_STRIDE * HEAD_DIM
    w1cat = lambda w: jnp.concatenate([w[:half], w[half:]], axis=1).astype(BF16)
    pe_rows = lambda pe: jnp.broadcast_to(pe.reshape(1, 2 * half), (16, 2 * half)).astype(BF16)
    w2k = jnp.pad(phi_k_w2, ((0, 0), (0, KA - HEAD_DIM))).astype(BF16)
    w2vt = jnp.pad(phi_v_w2.T, ((0, VA - HEAD_DIM), (0, 0))).astype(BF16)
    k_cmp, v_cmp_t = _compress(chunks(k_raw), chunks(v_raw), w1cat(phi_k_w1), w1cat(phi_v_w1),
                               pe_rows(cmp_pe_k), pe_rows(cmp_pe_v), w2k, w2vt)

    w_qg = b_w_qg[0]
    wq_t = w_qg[:, :N_HEADS * HEAD_DIM].T.astype(BF16)
    wg = w_qg[:, N_HEADS * HEAD_DIM:].reshape(D, N_GROUPS, HPG * 3)
    wg_t = jnp.pad(wg, ((0, 0), (0, 0), (0, GATE_ROWS - HPG * 3))).reshape(D, N_GROUPS * GATE_ROWS).T
    q_t, gates_t = _qproj(x, row(norm_mix[1]), wq_t, wg_t.astype(BF16), tm)

    tdiag, tprev, tcmp = _bias_tables(rel_bias)
    twin = jnp.asarray(np.tile(np.where(np.arange(QT)[:, None] > np.arange(QT)[None, :], 0.0, NEG_INF),
                               (1, HPG)).astype(np.float32))
    attn = _attention(q_t, gates_t, k_cmp, v_cmp_t, k_sel, v_sel_t, k_win, v_win_t, tdiag, tprev, twin, tcmp)

    out = _mlp(x.reshape(B * S, D), row(norm_mlp[1]), mlp_w1[1].astype(BF16), mlp_w2[1].astype(BF16), tm,
               attn=attn.reshape(B * S, D), w_o=b_w_o[0].astype(BF16), g_final=row(final_norm))
    return out.reshape(B, S, D)
```

```python
import functools
import math

import jax
import jax.numpy as jnp
from jax import lax
from jax.experimental import pallas as pl
from jax.experimental.pallas import tpu as pltpu

D_MODEL = 1024
N_HEADS = 16
HEAD_DIM = 64
N_GROUPS = 4
HPG = N_HEADS // N_GROUPS
D_FF = 4 * D_MODEL
CONV_WIDTH = 3
CMP_BLOCK = 32
CMP_STRIDE = 16
SEL_BLOCK = 64
SEL_TOP_N = 16
WINDOW = 512
PHI_HIDDEN = 256
N_BUCKETS = 32
MAX_DISTANCE = 128
EPS = 1e-6
NEG_INF = -1e30
FORCED_SCORE = 1e9
REMOVED_SCORE = -3e38

QT = 128
QL = HPG * QT
KA = 128
NSEL_MAX = 128
KSEL = KA + NSEL_MAX
VA = 80
FAR_KEYS = 1024
WIN_KEYS = WINDOW + QT
TAB_ROWS = WIN_KEYS + WINDOW
CMP_TAB = 24
LOG2E = 1.4426950408889634
VMEM_LIMIT = 56 * 1024 * 1024

F32 = jnp.float32
BF16 = jnp.bfloat16


def _dot(a, b):
    return jnp.dot(a, b, preferred_element_type=F32)


def _dot_nt(a, b):
    return lax.dot_general(a, b, (((1,), (1,)), ((), ())), preferred_element_type=F32)


def _rms(x, g):
    ms = jnp.mean(x * x, axis=-1, keepdims=True)
    return x * lax.rsqrt(ms + EPS) * g


def _resident(shape):
    zeros = (0,) * len(shape)
    return pl.BlockSpec(shape, lambda *_: zeros, pipeline_mode=pl.Buffered(1))


def _params(*sem):
    return pltpu.CompilerParams(dimension_semantics=sem, vmem_limit_bytes=VMEM_LIMIT)


CONV_CHUNK = 512


def _mix0_kernel(x_ref, g_ref, win_ref, cw_ref, wout_ref, o_ref, carry_ref, *, tm):
    @pl.when(pl.program_id(1) == 0)
    def _():
        carry_ref[...] = jnp.zeros_like(carry_ref)

    x = x_ref[0]
    h = _rms(x, g_ref[...]).astype(BF16)
    row = lax.broadcasted_iota(jnp.int32, (tm, CONV_CHUNK), 0)
    acc = jnp.zeros((tm, D_MODEL), F32)
    for c in range(D_MODEL // CONV_CHUNK):
        lo, hi = c * CONV_CHUNK, (c + 1) * CONV_CHUNK
        b_gate = _dot(h, win_ref[:, lo:hi])
        c_gate = _dot(h, win_ref[:, D_MODEL + lo:D_MODEL + hi])
        u = _dot(h, win_ref[:, 2 * D_MODEL + lo:2 * D_MODEL + hi])
        v = c_gate * u
        prev = carry_ref[:, lo:hi]
        v1 = jnp.where(row == 0, prev[7:8], pltpu.roll(v, 1, axis=0))
        v2 = pltpu.roll(v, 2, axis=0)
        v2 = jnp.where(row == 0, prev[6:7], jnp.where(row == 1, prev[7:8], v2))
        carry_ref[:, lo:hi] = v[tm - 8:tm]
        conv = cw_ref[0:1, lo:hi] * v2 + cw_ref[1:2, lo:hi] * v1 + cw_ref[2:3, lo:hi] * v
        acc = acc + _dot((b_gate * conv).astype(BF16), wout_ref[lo:hi, :])
    o_ref[0] = x + acc


def _mix0(x, g, w_in, conv_w, w_out, tm):
    B, S, D = x.shape
    return pl.pallas_call(
        functools.partial(_mix0_kernel, tm=tm),
        out_shape=jax.ShapeDtypeStruct((B, S, D), F32),
        grid=(B, S // tm),
        in_specs=[
            pl.BlockSpec((1, tm, D), lambda b, j: (b, j, 0)),
            _resident((1, D)),
            _resident((D, 3 * D)),
            _resident((CONV_WIDTH, D)),
            _resident((D, D)),
        ],
        out_specs=pl.BlockSpec((1, tm, D), lambda b, j: (b, j, 0)),
        scratch_shapes=[pltpu.VMEM((8, D), F32)],
        compiler_params=_params("arbitrary", "arbitrary"),
        name="mix0",
    )(x, g, w_in, conv_w, w_out)


FF_CHUNK = 1024


def _mlp_kernel(*refs, has_proj, has_final):
    refs = list(refs)
    x_ref = refs.pop(0)
    if has_proj:
        a_ref, wo_ref = refs.pop(0), refs.pop(0)
    g_ref, w1_ref, w2_ref = refs.pop(0), refs.pop(0), refs.pop(0)
    if has_final:
        gf_ref = refs.pop(0)
    (o_ref,) = refs

    x = x_ref[...]
    if has_proj:
        x = x + _dot(a_ref[...], wo_ref[...])
    h = _rms(x, g_ref[...]).astype(BF16)
    acc = jnp.zeros_like(x)
    for c in range(D_FF // FF_CHUNK):
        lo, hi = c * FF_CHUNK, (c + 1) * FF_CHUNK
        a = jnp.maximum(_dot(h, w1_ref[:, lo:hi]), 0.0)
        acc = acc + _dot((a * a).astype(BF16), w2_ref[lo:hi, :])
    y = x + acc
    if has_final:
        y = _rms(y, gf_ref[...])
    o_ref[...] = y


def _mlp(x, g, w1, w2, tm, attn=None, w_o=None, g_final=None):
    T, D = x.shape
    has_proj, has_final = attn is not None, g_final is not None
    row = pl.BlockSpec((tm, D), lambda i: (i, 0))
    args, specs = [x], [row]
    if has_proj:
        args += [attn, w_o]
        specs += [row, _resident((D, D))]
    args += [g, w1, w2]
    specs += [_resident((1, D)), _resident((D, D_FF)), _resident((D_FF, D))]
    if has_final:
        args.append(g_final)
        specs.append(_resident((1, D)))
    return pl.pallas_call(
        functools.partial(_mlp_kernel, has_proj=has_proj, has_final=has_final),
        out_shape=jax.ShapeDtypeStruct((T, D), F32),
        grid=(T // tm,),
        in_specs=specs,
        out_specs=row,
        compiler_params=_params("arbitrary"),
        name="mlp_out" if has_proj else "mlp",
    )(*args)


TOK_COLS = 2 * N_GROUPS * HEAD_DIM + 2 * N_GROUPS * KA
FEAT_ROWS = 2 * N_GROUPS * VA


def _kvproj_kernel(x_ref, g_ref, wtok_ref, wfeat_ref,
                   kraw_ref, vraw_ref, ksel_ref, kwin_ref, vsel_ref, vwin_ref, *, tm):
    gd = N_GROUPS * HEAD_DIM
    h = _rms(x_ref[0], g_ref[...]).astype(BF16)
    tok = _dot(h, wtok_ref[...])
    kraw_ref[0] = tok[:, 0:gd].astype(BF16)
    vraw_ref[0] = tok[:, gd:2 * gd].astype(BF16)
    pos = pl.program_id(1) * tm + lax.broadcasted_iota(jnp.int32, (tm, NSEL_MAX), 0)
    lane = lax.broadcasted_iota(jnp.int32, (tm, NSEL_MAX), 1)
    onehot = (lane == jnp.right_shift(pos, int(math.log2(SEL_BLOCK)))).astype(BF16)
    for g in range(N_GROUPS):
        base = 2 * gd + g * KA
        ksel_ref[0, g, :, 0:KA] = tok[:, base:base + KA].astype(BF16)
        ksel_ref[0, g, :, KA:KSEL] = onehot
        base = 2 * gd + N_GROUPS * KA + g * KA
        kwin_ref[0, g] = tok[:, base:base + KA].astype(BF16)
    feat = _dot_nt(wfeat_ref[...], h)
    ones_row = (lax.broadcasted_iota(jnp.int32, (VA, tm), 0) == HEAD_DIM).astype(F32)
    for g in range(N_GROUPS):
        vsel_ref[0, g] = (feat[g * VA:(g + 1) * VA] + ones_row).astype(BF16)
        base = N_GROUPS * VA + g * VA
        vwin_ref[0, g] = (feat[base:base + VA] + ones_row).astype(BF16)


def _kvproj(x, g, w_tok, w_feat, tm):
    B, S, D = x.shape
    gd = N_GROUPS * HEAD_DIM
    tokmaj = lambda w: pl.BlockSpec((1, tm, w), lambda b, j: (b, j, 0))
    return pl.pallas_call(
        functools.partial(_kvproj_kernel, tm=tm),
        out_shape=[
            jax.ShapeDtypeStruct((B, S, gd), BF16),
            jax.ShapeDtypeStruct((B, S, gd), BF16),
            jax.ShapeDtypeStruct((B, N_GROUPS, S, KSEL), BF16),
            jax.ShapeDtypeStruct((B, N_GROUPS, S, KA), BF16),
            jax.ShapeDtypeStruct((B, N_GROUPS, VA, S), BF16),
            jax.ShapeDtypeStruct((B, N_GROUPS, VA, S), BF16),
        ],
        grid=(B, S // tm),
        in_specs=[
            pl.BlockSpec((1, tm, D), lambda b, j: (b, j, 0)),
            _resident((1, D)),
            _resident((D, TOK_COLS)),
            _resident((FEAT_ROWS, D)),
        ],
        out_specs=[
            tokmaj(gd), tokmaj(gd),
            pl.BlockSpec((1, N_GROUPS, tm, KSEL), lambda b, j: (b, 0, j, 0)),
            pl.BlockSpec((1, N_GROUPS, tm, KA), lambda b, j: (b, 0, j, 0)),
            pl.BlockSpec((1, N_GROUPS, VA, tm), lambda b, j: (b, 0, 0, j)),
            pl.BlockSpec((1, N_GROUPS, VA, tm), lambda b, j: (b, 0, 0, j)),
        ],
        compiler_params=_params("arbitrary", "arbitrary"),
        name="kvproj",
    )(x, g, w_tok, w_feat)


def _compress_kernel(tk_ref, tv_ref, w1k_ref, w1v_ref, pek_ref, pev_ref, w2k_ref, w2vt_ref,
                     kc_ref, vct_ref, *, nc):
    half = CMP_STRIDE * HEAD_DIM

    def hidden(t_ref, w1_ref, pe_ref):
        ab = _dot(t_ref[0, 0], w1_ref[...])
        pe = (_dot(pe_ref[:, 0:half], w1_ref[:, 0:PHI_HIDDEN])
              + _dot(pe_ref[:, half:2 * half], w1_ref[:, PHI_HIDDEN:2 * PHI_HIDDEN]))
        nxt = pltpu.roll(ab[:, PHI_HIDDEN:2 * PHI_HIDDEN], nc - 1, axis=0)
        z = ab[:, 0:PHI_HIDDEN] + nxt + pe[0:1]
        return (z * jax.nn.sigmoid(z)).astype(BF16)

    kc = _dot(hidden(tk_ref, w1k_ref, pek_ref), w2k_ref[...])
    real = lax.broadcasted_iota(jnp.int32, (nc, KA), 0) < nc - 1
    kc_ref[0, 0] = jnp.where(real, kc, 0.0).astype(BF16)
    vct = _dot_nt(w2vt_ref[...], hidden(tv_ref, w1v_ref, pev_ref))
    real = lax.broadcasted_iota(jnp.int32, (VA, nc), 1) < nc - 1
    ones_row = (lax.broadcasted_iota(jnp.int32, (VA, nc), 0) == HEAD_DIM).astype(F32)
    vct_ref[0, 0] = (jnp.where(real, vct, 0.0) + ones_row).astype(BF16)


def _compress(tk, tv, w1k, w1v, pek, pev, w2k, w2vt):
    B, G, nc, width = tk.shape
    blk = pl.BlockSpec((1, 1, nc, width), lambda b, g: (b, g, 0, 0))
    return pl.pallas_call(
        functools.partial(_compress_kernel, nc=nc),
        out_shape=[
            jax.ShapeDtypeStruct((B, G, nc, KA), BF16),
            jax.ShapeDtypeStruct((B, G, VA, nc), BF16),
        ],
        grid=(B, G),
        in_specs=[
            blk, blk,
            _resident(w1k.shape), _resident(w1v.shape),
            _resident(pek.shape), _resident(pev.shape),
            _resident(w2k.shape), _resident(w2vt.shape),
        ],
        out_specs=[
            pl.BlockSpec((1, 1, nc, KA), lambda b, g: (b, g, 0, 0)),
            pl.BlockSpec((1, 1, VA, nc), lambda b, g: (b, g, 0, 0)),
        ],
        compiler_params=_params("arbitrary", "arbitrary"),
        name="compress",
    )(tk, tv, w1k, w1v, pek, pev, w2k, w2vt)


GATE_ROWS = 16


def _qproj_kernel(x_ref, g_ref, wq_ref, wg_ref, q_ref, gate_ref):
    h = _rms(x_ref[0], g_ref[...]).astype(BF16)
    q_ref[0] = (_dot_nt(wq_ref[...], h) * (HEAD_DIM ** -0.5 * LOG2E)).astype(BF16)
    gate_ref[0] = jax.nn.sigmoid(_dot_nt(wg_ref[...], h))


def _qproj(x, g, wq_t, wg_t, tm):
    B, S, D = x.shape
    hd = N_HEADS * HEAD_DIM
    gr = N_GROUPS * GATE_ROWS
    return pl.pallas_call(
        _qproj_kernel,
        out_shape=[
            jax.ShapeDtypeStruct((B, hd, S), BF16),
            jax.ShapeDtypeStruct((B, gr, S), F32),
        ],
        grid=(B, S // tm),
        in_specs=[
            pl.BlockSpec((1, tm, D), lambda b, j: (b, j, 0)),
            _resident((1, D)),
            _resident((hd, D)),
            _resident((gr, D)),
        ],
        out_specs=[
            pl.BlockSpec((1, hd, tm), lambda b, j: (b, 0, j)),
            pl.BlockSpec((1, gr, tm), lambda b, j: (b, 0, j)),
        ],
        compiler_params=_params("arbitrary", "arbitrary"),
        name="qproj",
    )(x, g, wq_t, wg_t)


def _t5_bucket(dist):
    max_exact = N_BUCKETS // 2
    logd = jnp.log(jnp.maximum(dist, 1).astype(F32) / max_exact)
    large = max_exact + (logd / math.log(MAX_DISTANCE / max_exact)
                         * (N_BUCKETS - max_exact)).astype(jnp.int32)
    large = jnp.minimum(large, N_BUCKETS - 1)
    return jnp.where(dist < 0, -1, jnp.where(dist < max_exact, dist, large))


def _bucket_tables():
    qry = jnp.arange(QT, dtype=jnp.int32)[None, :]
    dist = qry + WINDOW - jnp.arange(TAB_ROWS, dtype=jnp.int32)[:, None]
    win = jnp.where(dist < WINDOW, _t5_bucket(dist), -1)
    blk = jnp.arange(CMP_TAB, dtype=jnp.int32)[:, None] - (CMP_TAB - QT // CMP_STRIDE)
    cmp_ = _t5_bucket(qry - CMP_STRIDE * blk - (CMP_BLOCK - 1))
    return win, cmp_


def _bias_kernel(rb_ref, win_ref, cmp_ref, twin_ref, tcmp_ref):
    g = pl.program_id(0)

    def table(bucket_ref, out_ref):
        bucket = bucket_ref[...]
        for hg in range(HPG):
            head = g * HPG + hg
            t = jnp.zeros(bucket.shape, F32)
            for b in range(N_BUCKETS):
                t = jnp.where(bucket == b, rb_ref[b, head], t)
            t = (t - rb_ref[N_BUCKETS - 1, head]) * LOG2E
            out_ref[0, :, hg * QT:(hg + 1) * QT] = jnp.where(bucket >= 0, t, NEG_INF)

    table(win_ref, twin_ref)
    table(cmp_ref, tcmp_ref)


def _bias_tables(rel_bias):
    win, cmp_ = _bucket_tables()
    whole = lambda a: pl.BlockSpec(a.shape, lambda g: (0, 0))
    return pl.pallas_call(
        _bias_kernel,
        out_shape=[
            jax.ShapeDtypeStruct((N_GROUPS, TAB_ROWS, QL), F32),
            jax.ShapeDtypeStruct((N_GROUPS, CMP_TAB, QL), F32),
        ],
        grid=(N_GROUPS,),
        in_specs=[pl.BlockSpec(memory_space=pltpu.SMEM), whole(win), whole(cmp_)],
        out_specs=[
            pl.BlockSpec((1, TAB_ROWS, QL), lambda g: (g, 0, 0)),
            pl.BlockSpec((1, CMP_TAB, QL), lambda g: (g, 0, 0)),
        ],
        compiler_params=_params("arbitrary"),
        name="bias_tables",
    )(rel_bias, win, cmp_)


SC_PAD = 16
IMP_PAD = 8
M_INIT = -1e29


def _attn_kernel(q_ref, gate_ref, kc_ref, vct_ref, ks_ref, vst_ref, kw_ref, vwt_ref, twin_ref, tcmp_ref, o_ref,
                 wn_ref, wf_ref, sc_ref, imp_ref, accs_ref, ms_ref, sa_ref, sb_ref, ma_ref, mb_ref, *, nc):
    qt = pl.program_id(2)
    t0 = pl.multiple_of(qt * QT, QT)
    nsel = nc // (SEL_BLOCK // CMP_STRIDE)
    lanes = lambda hg: slice(hg * QT, (hg + 1) * QT)

    for hg in range(HPG):
        wn_ref[0:HEAD_DIM, lanes(hg)] = q_ref[0, hg * HEAD_DIM:(hg + 1) * HEAD_DIM, :]
    wn_ref[HEAD_DIM:KA, :] = jnp.zeros((KA - HEAD_DIM, QL), BF16)

    sc_ref[0:SC_PAD, :] = jnp.zeros((SC_PAD, QL), F32)
    sc_ref[SC_PAD:SC_PAD + nc, :] = _dot(kc_ref[0, 0], wn_ref[0:KA, :])
    win = pl.ds(pl.multiple_of(qt * (QT // CMP_STRIDE), 8), CMP_TAB)
    sc_ref[win, :] = sc_ref[win, :] + tcmp_ref[0]
    s = sc_ref[SC_PAD:SC_PAD + nc, :]
    blk = lax.broadcasted_iota(jnp.int32, (nc, QL), 0)
    s = jnp.where(blk < (qt + 1) * (QT // CMP_STRIDE), s, NEG_INF)
    m = jnp.max(s, axis=0, keepdims=True)
    e = jnp.exp2(s - m)
    has_any = m > 0.5 * NEG_INF
    p = e * jnp.where(has_any, 1.0 / jnp.sum(e, axis=0, keepdims=True), 0.0)
    o_cmp = _dot(vct_ref[0, 0], p.astype(BF16))

    def table(first_key, n):
        row0 = pl.multiple_of(first_key - (t0 - WINDOW), QT)
        return twin_ref[0, pl.ds(row0, n), :]

    wstart = pl.multiple_of(jnp.maximum(t0 - WINDOW, 0), QT)
    s_win = _dot(kw_ref[0, 0, pl.ds(wstart, WIN_KEYS), :], wn_ref[0:KA, :]) + table(wstart, WIN_KEYS)
    p_win = jnp.exp2(s_win - jnp.max(s_win, axis=0, keepdims=True)).astype(BF16)
    acc_win = _dot(vwt_ref[0, 0, :, pl.ds(wstart, WIN_KEYS)], p_win)

    imp_ref[0:IMP_PAD, :] = jnp.zeros((IMP_PAD, QT), F32)
    imp_ref[IMP_PAD:IMP_PAD + nc, :] = p[:, lanes(0)] + p[:, lanes(1)] + p[:, lanes(2)] + p[:, lanes(3)]
    imp_ref[IMP_PAD + nc:IMP_PAD + nc + 8, :] = jnp.zeros((8, QT), F32)
    ratio = SEL_BLOCK // CMP_STRIDE
    part = lambda r: imp_ref[pl.ds(IMP_PAD + r, nsel, stride=ratio), :]
    imp_s = part(-1) + 2.0 * (part(0) + part(1) + part(2)) + part(3)

    jb = lax.broadcasted_iota(jnp.int32, (nsel, QT), 0)
    cb = 2 * qt + (lax.broadcasted_iota(jnp.int32, (nsel, QT), 1) >= SEL_BLOCK).astype(jnp.int32)
    forced = (jb == 0) | (jb == cb) | (jb == cb - 1)
    score = jnp.where(forced, FORCED_SCORE, imp_s)
    score = jnp.where(jb > cb, NEG_INF, score)
    jbf = jb.astype(F32)

    def take_one(_, carry):
        score, sel = carry
        top = jnp.max(score, axis=0, keepdims=True)
        first = jnp.min(jnp.where(score == top, jbf, float(nsel)), axis=0, keepdims=True)
        pick = jbf == first
        sel = jnp.where(pick & (top > 0.5 * NEG_INF), 1.0, sel)
        return jnp.where(pick, REMOVED_SCORE, score), sel

    _, sel = lax.fori_loop(0, SEL_TOP_N, take_one, (score, jnp.zeros((nsel, QT), F32)), unroll=True)
    mask = jnp.where(sel > 0.0, 0.0, NEG_INF)
    mask_far = jnp.where(jb >= 2 * qt - 2, NEG_INF, mask).astype(BF16)
    mask = mask.astype(BF16)
    for hg in range(HPG):
        wn_ref[KA:KA + nsel, lanes(hg)] = mask
        wf_ref[KA:KA + nsel, lanes(hg)] = mask_far
    if nsel < NSEL_MAX:
        wn_ref[KA + nsel:KSEL, :] = jnp.zeros((NSEL_MAX - nsel, QL), BF16)
        wf_ref[KA + nsel:KSEL, :] = jnp.zeros((NSEL_MAX - nsel, QL), BF16)
    wf_ref[0:KA, :] = wn_ref[0:KA, :]

    def accumulate(acc_ref, m_ref, s, s_max, vt):
        m_old = m_ref[0:1, :]
        m_new = jnp.maximum(m_old, s_max)
        p = jnp.exp2(s - m_new).astype(BF16)
        acc_ref[...] = jnp.exp2(m_old - m_new) * acc_ref[...] + _dot(vt, p)
        m_ref[0:1, :] = m_new

    def reset(acc_ref, m_ref):
        acc_ref[...] = jnp.zeros_like(acc_ref)
        m_ref[...] = jnp.full(m_ref.shape, M_INIT, F32)

    def far_scores(tile, s_ref, smax_ref):
        start = pl.multiple_of(tile * FAR_KEYS, FAR_KEYS)
        s = _dot(ks_ref[0, 0, pl.ds(start, FAR_KEYS), :], wf_ref[...])
        s_ref[...] = s
        smax_ref[0:1, :] = jnp.max(s, axis=0, keepdims=True)

    def far_accumulate(tile, s_ref, smax_ref):
        start = pl.multiple_of(tile * FAR_KEYS, FAR_KEYS)
        accumulate(accs_ref, ms_ref, s_ref[...], smax_ref[0:1, :], vst_ref[0, 0, :, pl.ds(start, FAR_KEYS)])

    last_tile = nsel * SEL_BLOCK // FAR_KEYS - 1

    reset(accs_ref, ms_ref)
    near = pl.multiple_of(jnp.maximum(t0 - QT, 0), QT)
    s_near = _dot(ks_ref[0, 0, pl.ds(near, 2 * QT), :], wn_ref[...]) + table(near, 2 * QT)
    far_scores(0, sa_ref, ma_ref)
    accumulate(accs_ref, ms_ref, s_near, jnp.max(s_near, axis=0, keepdims=True),
               vst_ref[0, 0, :, pl.ds(near, 2 * QT)])

    def far_pair(i, _):
        far_scores(2 * i + 1, sb_ref, mb_ref)
        far_accumulate(2 * i, sa_ref, ma_ref)
        far_scores(jnp.minimum(2 * i + 2, last_tile), sa_ref, ma_ref)
        far_accumulate(2 * i + 1, sb_ref, mb_ref)
        return 0

    per = FAR_KEYS // QT
    lax.fori_loop(0, ((qt + per - 2) // per + 1) // 2, far_pair, 0)
    acc_sel = accs_ref[...]

    o_sel = acc_sel[0:HEAD_DIM, :] * (1.0 / acc_sel[HEAD_DIM:HEAD_DIM + 1, :])
    o_win = acc_win[0:HEAD_DIM, :] * (1.0 / acc_win[HEAD_DIM:HEAD_DIM + 1, :])
    heads = []
    for hg in range(HPG):
        gate = lambda br: gate_ref[0, hg * 3 + br:hg * 3 + br + 1, :]
        heads.append(gate(0) * o_cmp[0:HEAD_DIM, lanes(hg)] + gate(1) * o_sel[:, lanes(hg)]
                     + gate(2) * o_win[:, lanes(hg)])
    for pair in range(HPG // 2):
        both = jnp.concatenate(heads[2 * pair:2 * pair + 2], axis=0)
        o_ref[0, :, pair * 2 * HEAD_DIM:(pair + 1) * 2 * HEAD_DIM] = both.T.astype(BF16)


def _attention(q_t, gates_t, kc, vct, ks, vst, kw, vwt, twin, tcmp):
    B, hd, S = q_t.shape
    nc = kc.shape[2]
    per_bg = lambda *blk: pl.BlockSpec((1, 1) + blk, lambda b, g, j: (b, g, 0, 0))
    per_g = lambda rows: pl.BlockSpec((1, rows, QL), lambda b, g, j: (g, 0, 0))
    return pl.pallas_call(
        functools.partial(_attn_kernel, nc=nc),
        out_shape=jax.ShapeDtypeStruct((B, S, hd), BF16),
        grid=(B, N_GROUPS, S // QT),
        in_specs=[
            pl.BlockSpec((1, HPG * HEAD_DIM, QT), lambda b, g, j: (b, g, j)),
            pl.BlockSpec((1, GATE_ROWS, QT), lambda b, g, j: (b, g, j)),
            per_bg(nc, KA), per_bg(VA, nc),
            per_bg(S, KSEL), per_bg(VA, S),
            per_bg(S, KA), per_bg(VA, S),
            per_g(TAB_ROWS), per_g(CMP_TAB),
        ],
        out_specs=pl.BlockSpec((1, QT, HPG * HEAD_DIM), lambda b, g, j: (b, j, g)),
        scratch_shapes=[
            pltpu.VMEM((KSEL, QL), BF16),
            pltpu.VMEM((KSEL, QL), BF16),
            pltpu.VMEM((SC_PAD + nc, QL), F32),
            pltpu.VMEM((IMP_PAD + nc + 8, QT), F32),
            pltpu.VMEM((VA, QL), F32), pltpu.VMEM((8, QL), F32),
            pltpu.VMEM((FAR_KEYS, QL), F32), pltpu.VMEM((FAR_KEYS, QL), F32),
            pltpu.VMEM((8, QL), F32), pltpu.VMEM((8, QL), F32),
        ],
        compiler_params=_params("arbitrary", "arbitrary", "arbitrary"),
        name="nsa_attention",
    )(q_t, gates_t, kc, vct, ks, vst, kw, vwt, twin, tcmp)


def _pad_groups(w, width):
    d = w.shape[0]
    w = w.reshape(d, N_GROUPS, HEAD_DIM)
    return jnp.pad(w, ((0, 0), (0, 0), (0, width - HEAD_DIM))).reshape(d, N_GROUPS * width)


def kernel(x, rel_bias, norm_mix, norm_mlp, mlp_w1, mlp_w2, a_w_in, a_conv_w, a_w_out, kv_norm, w_kv,
           cmp_pe_k, cmp_pe_v, phi_k_w1, phi_k_w2, phi_v_w1, phi_v_w2, b_w_qg, b_w_o, final_norm):
    B, S, D = x.shape
    assert D == D_MODEL and S % (2 * FAR_KEYS) == 0 and S <= NSEL_MAX * SEL_BLOCK and norm_mix.shape[0] == 2
    tm = min(512, S)
    nc = S // CMP_STRIDE
    gd = N_GROUPS * HEAD_DIM
    row = lambda v: v.reshape(1, D).astype(F32)

    x = _mix0(x, row(norm_mix[0]), a_w_in[0].astype(BF16), a_conv_w[0], a_w_out[0].astype(BF16), tm)
    x = _mlp(x.reshape(B * S, D), row(norm_mlp[0]), mlp_w1[0].astype(BF16), mlp_w2[0].astype(BF16), tm)
    x = x.reshape(B, S, D)

    part = lambda c: w_kv[:, c * gd:(c + 1) * gd]
    w_tok = jnp.concatenate([part(0), part(1), _pad_groups(part(2), KA), _pad_groups(part(4), KA)], axis=1)
    w_feat = jnp.concatenate([_pad_groups(part(3), VA).T, _pad_groups(part(5), VA).T], axis=0)
    k_raw, v_raw, k_sel, k_win, v_sel_t, v_win_t = _kvproj(
        x, row(kv_norm), w_tok.astype(BF16), w_feat.astype(BF16), tm)

    def chunks(t):
        t = t.reshape(B, nc, CMP_STRIDE, N_GROUPS, HEAD_DIM)
        return jnp.transpose(t, (0, 3, 1, 2, 4)).reshape(B, N_GROUPS, nc, CMP_STRIDE * HEAD_DIM)

    half = CMP_STRIDE * HEAD_DIM
    w1cat = lambda w: jnp.concatenate([w[:half], w[half:]], axis=1).astype(BF16)
    pe_rows = lambda pe: jnp.broadcast_to(pe.reshape(1, 2 * half), (16, 2 * half)).astype(BF16)
    w2k = jnp.pad(phi_k_w2, ((0, 0), (0, KA - HEAD_DIM))).astype(BF16)
    w2vt = jnp.pad(phi_v_w2.T, ((0, VA - HEAD_DIM), (0, 0))).astype(BF16)
    k_cmp, v_cmp_t = _compress(chunks(k_raw), chunks(v_raw), w1cat(phi_k_w1), w1cat(phi_v_w1),
                               pe_rows(cmp_pe_k), pe_rows(cmp_pe_v), w2k, w2vt)

    w_qg = b_w_qg[0]
    wq_t = w_qg[:, :N_HEADS * HEAD_DIM].T.astype(BF16)
    wg = w_qg[:, N_HEADS * HEAD_DIM:].reshape(D, N_GROUPS, HPG * 3)
    wg_t = jnp.pad(wg, ((0, 0), (0, 0), (0, GATE_ROWS - HPG * 3))).reshape(D, N_GROUPS * GATE_ROWS).T
    q_t, gates_t = _qproj(x, row(norm_mix[1]), wq_t, wg_t.astype(BF16), tm)

    twin, tcmp = _bias_tables(rel_bias)
    attn = _attention(q_t, gates_t, k_cmp, v_cmp_t, k_sel, v_sel_t, k_win, v_win_t, twin, tcmp)

    out = _mlp(x.reshape(B * S, D), row(norm_mlp[1]), mlp_w1[1].astype(BF16), mlp_w2[1].astype(BF16), tm,
               attn=attn.reshape(B * S, D), w_o=b_w_o[0].astype(BF16), g_final=row(final_norm))
    return out.reshape(B, S, D)
```

```python
import functools
import math

import jax
import jax.numpy as jnp
from jax import lax
from jax.experimental import pallas as pl
from jax.experimental.pallas import tpu as pltpu

D_MODEL = 1024
N_HEADS = 16
HEAD_DIM = 64
N_GROUPS = 4
HPG = N_HEADS // N_GROUPS
D_FF = 4 * D_MODEL
CONV_WIDTH = 3
CMP_BLOCK = 32
CMP_STRIDE = 16
SEL_BLOCK = 64
SEL_TOP_N = 16
WINDOW = 512
PHI_HIDDEN = 256
N_BUCKETS = 32
MAX_DISTANCE = 128
EPS = 1e-6
NEG_INF = -1e30
REMOVED_SCORE = -3e38

QT = 128
QL = HPG * QT
KA = 128
NSEL_MAX = 128
KSEL = KA + NSEL_MAX
VA = 80
FAR_KEYS = 1024
WIN_KEYS = WINDOW + QT
TAB_ROWS = WIN_KEYS + WINDOW
CMP_TAB = 24
LOG2E = 1.4426950408889634
VMEM_LIMIT = 56 * 1024 * 1024

F32 = jnp.float32
BF16 = jnp.bfloat16


def _dot(a, b):
    return jnp.dot(a, b, preferred_element_type=F32)


def _dot_nt(a, b):
    return lax.dot_general(a, b, (((1,), (1,)), ((), ())), preferred_element_type=F32)


def _rms(x, g):
    ms = jnp.mean(x * x, axis=-1, keepdims=True)
    return x * lax.rsqrt(ms + EPS) * g


def _resident(shape):
    zeros = (0,) * len(shape)
    return pl.BlockSpec(shape, lambda *_: zeros, pipeline_mode=pl.Buffered(1))


def _params(*sem):
    return pltpu.CompilerParams(dimension_semantics=sem, vmem_limit_bytes=VMEM_LIMIT)


CONV_CHUNK = 512


def _mix0_kernel(x_ref, g_ref, win_ref, cw_ref, wout_ref, o_ref, carry_ref, *, tm):
    @pl.when(pl.program_id(1) == 0)
    def _():
        carry_ref[...] = jnp.zeros_like(carry_ref)

    x = x_ref[0]
    h = _rms(x, g_ref[...]).astype(BF16)
    row = lax.broadcasted_iota(jnp.int32, (tm, CONV_CHUNK), 0)
    acc = jnp.zeros((tm, D_MODEL), F32)
    for c in range(D_MODEL // CONV_CHUNK):
        lo, hi = c * CONV_CHUNK, (c + 1) * CONV_CHUNK
        b_gate = _dot(h, win_ref[:, lo:hi])
        c_gate = _dot(h, win_ref[:, D_MODEL + lo:D_MODEL + hi])
        u = _dot(h, win_ref[:, 2 * D_MODEL + lo:2 * D_MODEL + hi])
        v = c_gate * u
        prev = carry_ref[:, lo:hi]
        v1 = jnp.where(row == 0, prev[7:8], pltpu.roll(v, 1, axis=0))
        v2 = pltpu.roll(v, 2, axis=0)
        v2 = jnp.where(row == 0, prev[6:7], jnp.where(row == 1, prev[7:8], v2))
        carry_ref[:, lo:hi] = v[tm - 8:tm]
        conv = cw_ref[0:1, lo:hi] * v2 + cw_ref[1:2, lo:hi] * v1 + cw_ref[2:3, lo:hi] * v
        acc = acc + _dot((b_gate * conv).astype(BF16), wout_ref[lo:hi, :])
    o_ref[0] = x + acc


def _mix0(x, g, w_in, conv_w, w_out, tm):
    B, S, D = x.shape
    return pl.pallas_call(
        functools.partial(_mix0_kernel, tm=tm),
        out_shape=jax.ShapeDtypeStruct((B, S, D), F32),
        grid=(B, S // tm),
        in_specs=[
            pl.BlockSpec((1, tm, D), lambda b, j: (b, j, 0)),
            _resident((1, D)),
            _resident((D, 3 * D)),
            _resident((CONV_WIDTH, D)),
            _resident((D, D)),
        ],
        out_specs=pl.BlockSpec((1, tm, D), lambda b, j: (b, j, 0)),
        scratch_shapes=[pltpu.VMEM((8, D), F32)],
        compiler_params=_params("arbitrary", "arbitrary"),
        name="mix0",
    )(x, g, w_in, conv_w, w_out)


FF_CHUNK = 1024


def _mlp_kernel(*refs, has_proj, has_final):
    refs = list(refs)
    x_ref = refs.pop(0)
    if has_proj:
        a_ref, wo_ref = refs.pop(0), refs.pop(0)
    g_ref, w1_ref, w2_ref = refs.pop(0), refs.pop(0), refs.pop(0)
    if has_final:
        gf_ref = refs.pop(0)
    (o_ref,) = refs

    x = x_ref[...]
    if has_proj:
        x = x + _dot(a_ref[...], wo_ref[...])
    h = _rms(x, g_ref[...]).astype(BF16)
    acc = jnp.zeros_like(x)
    for c in range(D_FF // FF_CHUNK):
        lo, hi = c * FF_CHUNK, (c + 1) * FF_CHUNK
        a = jnp.maximum(_dot(h, w1_ref[:, lo:hi]), 0.0)
        acc = acc + _dot((a * a).astype(BF16), w2_ref[lo:hi, :])
    y = x + acc
    if has_final:
        y = _rms(y, gf_ref[...])
    o_ref[...] = y


def _mlp(x, g, w1, w2, tm, attn=None, w_o=None, g_final=None):
    T, D = x.shape
    has_proj, has_final = attn is not None, g_final is not None
    row = pl.BlockSpec((tm, D), lambda i: (i, 0))
    args, specs = [x], [row]
    if has_proj:
        args += [attn, w_o]
        specs += [row, _resident((D, D))]
    args += [g, w1, w2]
    specs += [_resident((1, D)), _resident((D, D_FF)), _resident((D_FF, D))]
    if has_final:
        args.append(g_final)
        specs.append(_resident((1, D)))
    return pl.pallas_call(
        functools.partial(_mlp_kernel, has_proj=has_proj, has_final=has_final),
        out_shape=jax.ShapeDtypeStruct((T, D), F32),
        grid=(T // tm,),
        in_specs=specs,
        out_specs=row,
        compiler_params=_params("arbitrary"),
        name="mlp_out" if has_proj else "mlp",
    )(*args)


TOK_COLS = 2 * N_GROUPS * HEAD_DIM + 2 * N_GROUPS * KA
FEAT_ROWS = 2 * N_GROUPS * VA


def _kvproj_kernel(x_ref, g_ref, wtok_ref, wfeat_ref,
                   kraw_ref, vraw_ref, ksel_ref, kwin_ref, vsel_ref, vwin_ref, *, tm):
    gd = N_GROUPS * HEAD_DIM
    h = _rms(x_ref[0], g_ref[...]).astype(BF16)
    tok = _dot(h, wtok_ref[...])
    kraw_ref[0] = tok[:, 0:gd].astype(BF16)
    vraw_ref[0] = tok[:, gd:2 * gd].astype(BF16)
    pos = pl.program_id(1) * tm + lax.broadcasted_iota(jnp.int32, (tm, NSEL_MAX), 0)
    lane = lax.broadcasted_iota(jnp.int32, (tm, NSEL_MAX), 1)
    onehot = (lane == jnp.right_shift(pos, int(math.log2(SEL_BLOCK)))).astype(BF16)
    for g in range(N_GROUPS):
        base = 2 * gd + g * KA
        ksel_ref[0, g, :, 0:KA] = tok[:, base:base + KA].astype(BF16)
        ksel_ref[0, g, :, KA:KSEL] = onehot
        base = 2 * gd + N_GROUPS * KA + g * KA
        kwin_ref[0, g] = tok[:, base:base + KA].astype(BF16)
    feat = _dot_nt(wfeat_ref[...], h)
    ones_row = (lax.broadcasted_iota(jnp.int32, (VA, tm), 0) == HEAD_DIM).astype(F32)
    for g in range(N_GROUPS):
        vsel_ref[0, g] = (feat[g * VA:(g + 1) * VA] + ones_row).astype(BF16)
        base = N_GROUPS * VA + g * VA
        vwin_ref[0, g] = (feat[base:base + VA] + ones_row).astype(BF16)


def _kvproj(x, g, w_tok, w_feat, tm):
    B, S, D = x.shape
    gd = N_GROUPS * HEAD_DIM
    tokmaj = lambda w: pl.BlockSpec((1, tm, w), lambda b, j: (b, j, 0))
    return pl.pallas_call(
        functools.partial(_kvproj_kernel, tm=tm),
        out_shape=[
            jax.ShapeDtypeStruct((B, S, gd), BF16),
            jax.ShapeDtypeStruct((B, S, gd), BF16),
            jax.ShapeDtypeStruct((B, N_GROUPS, S, KSEL), BF16),
            jax.ShapeDtypeStruct((B, N_GROUPS, S, KA), BF16),
            jax.ShapeDtypeStruct((B, N_GROUPS, VA, S), BF16),
            jax.ShapeDtypeStruct((B, N_GROUPS, VA, S), BF16),
        ],
        grid=(B, S // tm),
        in_specs=[
            pl.BlockSpec((1, tm, D), lambda b, j: (b, j, 0)),
            _resident((1, D)),
            _resident((D, TOK_COLS)),
            _resident((FEAT_ROWS, D)),
        ],
        out_specs=[
            tokmaj(gd), tokmaj(gd),
            pl.BlockSpec((1, N_GROUPS, tm, KSEL), lambda b, j: (b, 0, j, 0)),
            pl.BlockSpec((1, N_GROUPS, tm, KA), lambda b, j: (b, 0, j, 0)),
            pl.BlockSpec((1, N_GROUPS, VA, tm), lambda b, j: (b, 0, 0, j)),
            pl.BlockSpec((1, N_GROUPS, VA, tm), lambda b, j: (b, 0, 0, j)),
        ],
        compiler_params=_params("arbitrary", "arbitrary"),
        name="kvproj",
    )(x, g, w_tok, w_feat)


def _compress_kernel(tk_ref, tv_ref, w1k_ref, w1v_ref, pek_ref, pev_ref, w2k_ref, w2vt_ref,
                     kc_ref, vct_ref, *, nc):
    half = CMP_STRIDE * HEAD_DIM

    def hidden(t_ref, w1_ref, pe_ref):
        ab = _dot(t_ref[0, 0], w1_ref[...])
        pe = (_dot(pe_ref[:, 0:half], w1_ref[:, 0:PHI_HIDDEN])
              + _dot(pe_ref[:, half:2 * half], w1_ref[:, PHI_HIDDEN:2 * PHI_HIDDEN]))
        nxt = pltpu.roll(ab[:, PHI_HIDDEN:2 * PHI_HIDDEN], nc - 1, axis=0)
        z = ab[:, 0:PHI_HIDDEN] + nxt + pe[0:1]
        return (z * jax.nn.sigmoid(z)).astype(BF16)

    kc = _dot(hidden(tk_ref, w1k_ref, pek_ref), w2k_ref[...])
    real = lax.broadcasted_iota(jnp.int32, (nc, KA), 0) < nc - 1
    kc_ref[0, 0] = jnp.where(real, kc, 0.0).astype(BF16)
    vct = _dot_nt(w2vt_ref[...], hidden(tv_ref, w1v_ref, pev_ref))
    real = lax.broadcasted_iota(jnp.int32, (VA, nc), 1) < nc - 1
    ones_row = (lax.broadcasted_iota(jnp.int32, (VA, nc), 0) == HEAD_DIM).astype(F32)
    vct_ref[0, 0] = (jnp.where(real, vct, 0.0) + ones_row).astype(BF16)


def _compress(tk, tv, w1k, w1v, pek, pev, w2k, w2vt):
    B, G, nc, width = tk.shape
    blk = pl.BlockSpec((1, 1, nc, width), lambda b, g: (b, g, 0, 0))
    return pl.pallas_call(
        functools.partial(_compress_kernel, nc=nc),
        out_shape=[
            jax.ShapeDtypeStruct((B, G, nc, KA), BF16),
            jax.ShapeDtypeStruct((B, G, VA, nc), BF16),
        ],
        grid=(B, G),
        in_specs=[
            blk, blk,
            _resident(w1k.shape), _resident(w1v.shape),
            _resident(pek.shape), _resident(pev.shape),
            _resident(w2k.shape), _resident(w2vt.shape),
        ],
        out_specs=[
            pl.BlockSpec((1, 1, nc, KA), lambda b, g: (b, g, 0, 0)),
            pl.BlockSpec((1, 1, VA, nc), lambda b, g: (b, g, 0, 0)),
        ],
        compiler_params=_params("arbitrary", "arbitrary"),
        name="compress",
    )(tk, tv, w1k, w1v, pek, pev, w2k, w2vt)


GATE_ROWS = 16


def _qproj_kernel(x_ref, g_ref, wq_ref, wg_ref, q_ref, gate_ref):
    h = _rms(x_ref[0], g_ref[...]).astype(BF16)
    q_ref[0] = (_dot_nt(wq_ref[...], h) * (HEAD_DIM ** -0.5 * LOG2E)).astype(BF16)
    gate_ref[0] = jax.nn.sigmoid(_dot_nt(wg_ref[...], h))


def _qproj(x, g, wq_t, wg_t, tm):
    B, S, D = x.shape
    hd = N_HEADS * HEAD_DIM
    gr = N_GROUPS * GATE_ROWS
    return pl.pallas_call(
        _qproj_kernel,
        out_shape=[
            jax.ShapeDtypeStruct((B, hd, S), BF16),
            jax.ShapeDtypeStruct((B, gr, S), F32),
        ],
        grid=(B, S // tm),
        in_specs=[
            pl.BlockSpec((1, tm, D), lambda b, j: (b, j, 0)),
            _resident((1, D)),
            _resident((hd, D)),
            _resident((gr, D)),
        ],
        out_specs=[
            pl.BlockSpec((1, hd, tm), lambda b, j: (b, 0, j)),
            pl.BlockSpec((1, gr, tm), lambda b, j: (b, 0, j)),
        ],
        compiler_params=_params("arbitrary", "arbitrary"),
        name="qproj",
    )(x, g, wq_t, wg_t)


def _t5_bucket(dist):
    max_exact = N_BUCKETS // 2
    logd = jnp.log(jnp.maximum(dist, 1).astype(F32) / max_exact)
    large = max_exact + (logd / math.log(MAX_DISTANCE / max_exact)
                         * (N_BUCKETS - max_exact)).astype(jnp.int32)
    large = jnp.minimum(large, N_BUCKETS - 1)
    return jnp.where(dist < 0, -1, jnp.where(dist < max_exact, dist, large))


def _bucket_tables():
    qry = jnp.arange(QT, dtype=jnp.int32)[None, :]
    dist = qry + WINDOW - jnp.arange(TAB_ROWS, dtype=jnp.int32)[:, None]
    win = jnp.where(dist < WINDOW, _t5_bucket(dist), -1)
    blk = jnp.arange(CMP_TAB, dtype=jnp.int32)[:, None] - (CMP_TAB - QT // CMP_STRIDE)
    cmp_ = _t5_bucket(qry - CMP_STRIDE * blk - (CMP_BLOCK - 1))
    return win, cmp_


def _bias_kernel(rb_ref, win_ref, cmp_ref, twin_ref, tcmp_ref):
    g = pl.program_id(0)

    def table(bucket_ref, out_ref):
        bucket = bucket_ref[...]
        for hg in range(HPG):
            head = g * HPG + hg
            t = jnp.zeros(bucket.shape, F32)
            for b in range(N_BUCKETS):
                t = jnp.where(bucket == b, rb_ref[b, head], t)
            t = (t - rb_ref[N_BUCKETS - 1, head]) * LOG2E
            out_ref[0, :, hg * QT:(hg + 1) * QT] = jnp.where(bucket >= 0, t, NEG_INF)

    table(win_ref, twin_ref)
    table(cmp_ref, tcmp_ref)


def _bias_tables(rel_bias):
    win, cmp_ = _bucket_tables()
    whole = lambda a: pl.BlockSpec(a.shape, lambda g: (0, 0))
    return pl.pallas_call(
        _bias_kernel,
        out_shape=[
            jax.ShapeDtypeStruct((N_GROUPS, TAB_ROWS, QL), F32),
            jax.ShapeDtypeStruct((N_GROUPS, CMP_TAB, QL), F32),
        ],
        grid=(N_GROUPS,),
        in_specs=[pl.BlockSpec(memory_space=pltpu.SMEM), whole(win), whole(cmp_)],
        out_specs=[
            pl.BlockSpec((1, TAB_ROWS, QL), lambda g: (g, 0, 0)),
            pl.BlockSpec((1, CMP_TAB, QL), lambda g: (g, 0, 0)),
        ],
        compiler_params=_params("arbitrary"),
        name="bias_tables",
    )(rel_bias, win, cmp_)


SC_PAD = 16
IMP_PAD = 8
M_INIT = -1e29
EXP2_HEADROOM = 100.0


def _attn_kernel(q_ref, gate_ref, kc_ref, vct_ref, ks_ref, vst_ref, kw_ref, vwt_ref, twin_ref, tcmp_ref, o_ref,
                 wn_ref, wf_ref, sc_ref, imp_ref, accs_ref, ms_ref, ex_ref, *, nc):
    qt = pl.program_id(2)
    t0 = pl.multiple_of(qt * QT, QT)
    nsel = nc // (SEL_BLOCK // CMP_STRIDE)
    lanes = lambda hg: slice(hg * QT, (hg + 1) * QT)

    for hg in range(HPG):
        wn_ref[0:HEAD_DIM, lanes(hg)] = q_ref[0, hg * HEAD_DIM:(hg + 1) * HEAD_DIM, :]
    wn_ref[HEAD_DIM:KA, :] = jnp.zeros((KA - HEAD_DIM, QL), BF16)

    def table(first_key, n):
        row0 = pl.multiple_of(first_key - (t0 - WINDOW), QT)
        return twin_ref[0, pl.ds(row0, n), :]

    sc_ref[0:SC_PAD, :] = jnp.zeros((SC_PAD, QL), F32)
    sc_ref[SC_PAD:SC_PAD + nc, :] = _dot(kc_ref[0, 0], wn_ref[0:KA, :])
    wstart = pl.multiple_of(jnp.maximum(t0 - WINDOW, 0), QT)
    s_win = _dot(kw_ref[0, 0, pl.ds(wstart, WIN_KEYS), :], wn_ref[0:KA, :]) + table(wstart, WIN_KEYS)
    win = pl.ds(pl.multiple_of(qt * (QT // CMP_STRIDE), 8), CMP_TAB)
    sc_ref[win, :] = sc_ref[win, :] + tcmp_ref[0]
    s = sc_ref[SC_PAD:SC_PAD + nc, :]
    blk = lax.broadcasted_iota(jnp.int32, (nc, QL), 0)
    s = jnp.where(blk < (qt + 1) * (QT // CMP_STRIDE), s, NEG_INF)
    m = jnp.max(s, axis=0, keepdims=True)
    e = jnp.exp2(s - m)
    has_any = m > 0.5 * NEG_INF
    p = e * jnp.where(has_any, 1.0 / jnp.sum(e, axis=0, keepdims=True), 0.0)
    o_cmp = _dot(vct_ref[0, 0], p.astype(BF16))

    p_win = jnp.exp2(s_win - jnp.max(s_win, axis=0, keepdims=True)).astype(BF16)
    acc_win = _dot(vwt_ref[0, 0, :, pl.ds(wstart, WIN_KEYS)], p_win)

    imp_ref[0:IMP_PAD, :] = jnp.zeros((IMP_PAD, QT), F32)
    imp_ref[IMP_PAD:IMP_PAD + nc, :] = p[:, lanes(0)] + p[:, lanes(1)] + p[:, lanes(2)] + p[:, lanes(3)]
    imp_ref[IMP_PAD + nc:IMP_PAD + nc + 8, :] = jnp.zeros((8, QT), F32)
    ratio = SEL_BLOCK // CMP_STRIDE
    part = lambda r: imp_ref[pl.ds(IMP_PAD + r, nsel, stride=ratio), :]
    imp_s = part(-1) + 2.0 * (part(0) + part(1) + part(2)) + part(3)

    jb = lax.broadcasted_iota(jnp.int32, (nsel, QT), 0)
    cb = 2 * qt + (lax.broadcasted_iota(jnp.int32, (nsel, QT), 1) >= SEL_BLOCK).astype(jnp.int32)
    forced = (jb == 0) | (jb == cb) | (jb == cb - 1)
    candidate = (jb <= cb) & jnp.logical_not(forced)
    score = jnp.where(candidate, imp_s, NEG_INF)
    jbf = jb.astype(F32)

    def take_one(_, score):
        top = jnp.max(score, axis=0, keepdims=True)
        first = jnp.min(jnp.where(score == top, jbf, float(nsel)), axis=0, keepdims=True)
        return jnp.where(jbf == first, REMOVED_SCORE, score)

    score = lax.fori_loop(0, SEL_TOP_N - 3, take_one, score, unroll=True)
    sel = (forced & (jb <= cb)) | (candidate & (score < 2.0 * NEG_INF))
    mask = jnp.where(sel, 0.0, NEG_INF)
    mask_far = jnp.where(jb >= 2 * qt - 2, NEG_INF, mask).astype(BF16)
    mask = mask.astype(BF16)
    for hg in range(HPG):
        wn_ref[KA:KA + nsel, lanes(hg)] = mask
        wf_ref[KA:KA + nsel, lanes(hg)] = mask_far
    if nsel < NSEL_MAX:
        wn_ref[KA + nsel:KSEL, :] = jnp.zeros((NSEL_MAX - nsel, QL), BF16)
        wf_ref[KA + nsel:KSEL, :] = jnp.zeros((NSEL_MAX - nsel, QL), BF16)
    wf_ref[0:KA, :] = wn_ref[0:KA, :]

    def accumulate(acc_ref, m_ref, s, s_max, vt):
        m_old = m_ref[0:1, :]
        m_new = jnp.maximum(m_old, s_max)
        p = jnp.exp2(s - m_new).astype(BF16)
        acc_ref[...] = jnp.exp2(m_old - m_new) * acc_ref[...] + _dot(vt, p)
        m_ref[0:1, :] = m_new

    def reset(acc_ref, m_ref):
        acc_ref[...] = jnp.zeros_like(acc_ref)
        m_ref[...] = jnp.full(m_ref.shape, M_INIT, F32)

    near = pl.multiple_of(jnp.maximum(t0 - QT, 0), QT)
    far_tiles = (qt + FAR_KEYS // QT - 2) // (FAR_KEYS // QT)

    def far_keys(tile):
        start = pl.multiple_of(tile * FAR_KEYS, FAR_KEYS)
        return ks_ref[0, 0, pl.ds(start, FAR_KEYS), :], vst_ref[0, 0, :, pl.ds(start, FAR_KEYS)]

    reset(accs_ref, ms_ref)
    s_near = _dot(ks_ref[0, 0, pl.ds(near, 2 * QT), :], wn_ref[...]) + table(near, 2 * QT)
    accumulate(accs_ref, ms_ref, s_near, jnp.max(s_near, axis=0, keepdims=True),
               vst_ref[0, 0, :, pl.ds(near, 2 * QT)])
    ref_level = ms_ref[0:1, :]

    def far_fast(tiles):
        acc, excess = accs_ref[...], ex_ref[0:1, :]
        for tile in tiles:
            k, vt = far_keys(tile)
            s = _dot(k, wf_ref[...]) - ref_level
            acc = acc + _dot(vt, jnp.exp2(s).astype(BF16))
            excess = jnp.maximum(excess, jnp.max(s, axis=0, keepdims=True))
        accs_ref[...] = acc
        ex_ref[0:1, :] = excess

    ex_ref[...] = jnp.full(ex_ref.shape, NEG_INF, F32)

    def far_four(i, _):
        far_fast((4 * i, 4 * i + 1, 4 * i + 2, 4 * i + 3))
        return 0

    lax.fori_loop(0, far_tiles // 4, far_four, 0)
    done = far_tiles // 4 * 4

    @pl.when(far_tiles % 4 >= 2)
    def _():
        far_fast((done, done + 1))

    @pl.when(far_tiles % 2 == 1)
    def _():
        far_fast((far_tiles - 1,))

    @pl.when(jnp.max(ex_ref[0:1, :]) > EXP2_HEADROOM)
    def _():
        reset(accs_ref, ms_ref)
        s_again = _dot(ks_ref[0, 0, pl.ds(near, 2 * QT), :], wn_ref[...]) + table(near, 2 * QT)
        accumulate(accs_ref, ms_ref, s_again, jnp.max(s_again, axis=0, keepdims=True),
                   vst_ref[0, 0, :, pl.ds(near, 2 * QT)])

        def far_tile_exact(tile, _):
            k, vt = far_keys(tile)
            s = _dot(k, wf_ref[...])
            accumulate(accs_ref, ms_ref, s, jnp.max(s, axis=0, keepdims=True), vt)
            return 0

        lax.fori_loop(0, far_tiles, far_tile_exact, 0)

    acc_sel = accs_ref[...]

    o_sel = acc_sel[0:HEAD_DIM, :] * (1.0 / acc_sel[HEAD_DIM:HEAD_DIM + 1, :])
    o_win = acc_win[0:HEAD_DIM, :] * (1.0 / acc_win[HEAD_DIM:HEAD_DIM + 1, :])
    heads = []
    for hg in range(HPG):
        gate = lambda br: gate_ref[0, hg * 3 + br:hg * 3 + br + 1, :]
        heads.append(gate(0) * o_cmp[0:HEAD_DIM, lanes(hg)] + gate(1) * o_sel[:, lanes(hg)]
                     + gate(2) * o_win[:, lanes(hg)])
    for pair in range(HPG // 2):
        both = jnp.concatenate(heads[2 * pair:2 * pair + 2], axis=0)
        o_ref[0, :, pair * 2 * HEAD_DIM:(pair + 1) * 2 * HEAD_DIM] = both.T.astype(BF16)


def _attention(q_t, gates_t, kc, vct, ks, vst, kw, vwt, twin, tcmp):
    B, hd, S = q_t.shape
    nc = kc.shape[2]
    per_bg = lambda *blk: pl.BlockSpec((1, 1) + blk, lambda b, g, j: (b, g, 0, 0))
    per_g = lambda rows: pl.BlockSpec((1, rows, QL), lambda b, g, j: (g, 0, 0))
    return pl.pallas_call(
        functools.partial(_attn_kernel, nc=nc),
        out_shape=jax.ShapeDtypeStruct((B, S, hd), BF16),
        grid=(B, N_GROUPS, S // QT),
        in_specs=[
            pl.BlockSpec((1, HPG * HEAD_DIM, QT), lambda b, g, j: (b, g, j)),
            pl.BlockSpec((1, GATE_ROWS, QT), lambda b, g, j: (b, g, j)),
            per_bg(nc, KA), per_bg(VA, nc),
            per_bg(S, KSEL), per_bg(VA, S),
            per_bg(S, KA), per_bg(VA, S),
            per_g(TAB_ROWS), per_g(CMP_TAB),
        ],
        out_specs=pl.BlockSpec((1, QT, HPG * HEAD_DIM), lambda b, g, j: (b, j, g)),
        scratch_shapes=[
            pltpu.VMEM((KSEL, QL), BF16),
            pltpu.VMEM((KSEL, QL), BF16),
            pltpu.VMEM((SC_PAD + nc, QL), F32),
            pltpu.VMEM((IMP_PAD + nc + 8, QT), F32),
            pltpu.VMEM((VA, QL), F32), pltpu.VMEM((8, QL), F32),
            pltpu.VMEM((8, QL), F32),
        ],
        compiler_params=_params("arbitrary", "arbitrary", "arbitrary"),
        name="nsa_attention",
    )(q_t, gates_t, kc, vct, ks, vst, kw, vwt, twin, tcmp)


def _pad_groups(w, width):
    d = w.shape[0]
    w = w.reshape(d, N_GROUPS, HEAD_DIM)
    return jnp.pad(w, ((0, 0), (0, 0), (0, width - HEAD_DIM))).reshape(d, N_GROUPS * width)


def kernel(x, rel_bias, norm_mix, norm_mlp, mlp_w1, mlp_w2, a_w_in, a_conv_w, a_w_out, kv_norm, w_kv,
           cmp_pe_k, cmp_pe_v, phi_k_w1, phi_k_w2, phi_v_w1, phi_v_w2, b_w_qg, b_w_o, final_norm):
    B, S, D = x.shape
    assert D == D_MODEL and S % (2 * FAR_KEYS) == 0 and S <= NSEL_MAX * SEL_BLOCK and norm_mix.shape[0] == 2
    tm = min(512, S)
    nc = S // CMP_STRIDE
    gd = N_GROUPS * HEAD_DIM
    row = lambda v: v.reshape(1, D).astype(F32)

    x = _mix0(x, row(norm_mix[0]), a_w_in[0].astype(BF16), a_conv_w[0], a_w_out[0].astype(BF16), tm)
    x = _mlp(x.reshape(B * S, D), row(norm_mlp[0]), mlp_w1[0].astype(BF16), mlp_w2[0].astype(BF16), tm)
    x = x.reshape(B, S, D)

    part = lambda c: w_kv[:, c * gd:(c + 1) * gd]
    w_tok = jnp.concatenate([part(0), part(1), _pad_groups(part(2), KA), _pad_groups(part(4), KA)], axis=1)
    w_feat = jnp.concatenate([_pad_groups(part(3), VA).T, _pad_groups(part(5), VA).T], axis=0)
    k_raw, v_raw, k_sel, k_win, v_sel_t, v_win_t = _kvproj(
        x, row(kv_norm), w_tok.astype(BF16), w_feat.astype(BF16), tm)

    def chunks(t):
        t = t.reshape(B, nc, CMP_STRIDE, N_GROUPS, HEAD_DIM)
        return jnp.transpose(t, (0, 3, 1, 2, 4)).reshape(B, N_GROUPS, nc, CMP_STRIDE * HEAD_DIM)

    half = CMP_STRIDE * HEAD_DIM
    w1cat = lambda w: jnp.concatenate([w[:half], w[half:]], axis=1).astype(BF16)
    pe_rows = lambda pe: jnp.broadcast_to(pe.reshape(1, 2 * half), (16, 2 * half)).astype(BF16)
    w2k = jnp.pad(phi_k_w2, ((0, 0), (0, KA - HEAD_DIM))).astype(BF16)
    w2vt = jnp.pad(phi_v_w2.T, ((0, VA - HEAD_DIM), (0, 0))).astype(BF16)
    k_cmp, v_cmp_t = _compress(chunks(k_raw), chunks(v_raw), w1cat(phi_k_w1), w1cat(phi_v_w1),
                               pe_rows(cmp_pe_k), pe_rows(cmp_pe_v), w2k, w2vt)

    w_qg = b_w_qg[0]
    wq_t = w_qg[:, :N_HEADS * HEAD_DIM].T.astype(BF16)
    wg = w_qg[:, N_HEADS * HEAD_DIM:].reshape(D, N_GROUPS, HPG * 3)
    wg_t = jnp.pad(wg, ((0, 0), (0, 0), (0, GATE_ROWS - HPG * 3))).reshape(D, N_GROUPS * GATE_ROWS).T
    q_t, gates_t = _qproj(x, row(norm_mix[1]), wq_t, wg_t.astype(BF16), tm)

    twin, tcmp = _bias_tables(rel_bias)
    attn = _attention(q_t, gates_t, k_cmp, v_cmp_t, k_sel, v_sel_t, k_win, v_win_t, twin, tcmp)

    out = _mlp(x.reshape(B * S, D), row(norm_mlp[1]), mlp_w1[1].astype(BF16), mlp_w2[1].astype(BF16), tm,
               attn=attn.reshape(B * S, D), w_o=b_w_o[0].astype(BF16), g_final=row(final_norm))
    return out.reshape(B, S, D)
```

```python
import functools
import math

import jax
import jax.numpy as jnp
from jax import lax
from jax.experimental import pallas as pl
from jax.experimental.pallas import tpu as pltpu

D_MODEL = 1024
N_HEADS = 16
HEAD_DIM = 64
N_GROUPS = 4
HPG = N_HEADS // N_GROUPS
D_FF = 4 * D_MODEL
CONV_WIDTH = 3
CMP_BLOCK = 32
CMP_STRIDE = 16
SEL_BLOCK = 64
SEL_TOP_N = 16
WINDOW = 512
PHI_HIDDEN = 256
N_BUCKETS = 32
MAX_DISTANCE = 128
EPS = 1e-6
NEG_INF = -1e30
REMOVED_SCORE = -3e38

QT = 128
QL = HPG * QT
KA = 128
NSEL_MAX = 128
KSEL = KA + NSEL_MAX
VA = 80
FAR_KEYS = 1024
WIN_KEYS = WINDOW + QT
TAB_ROWS = WIN_KEYS + WINDOW
CMP_TAB = 24
LOG2E = 1.4426950408889634
VMEM_LIMIT = 56 * 1024 * 1024

F32 = jnp.float32
BF16 = jnp.bfloat16


def _dot(a, b):
    return jnp.dot(a, b, preferred_element_type=F32)


def _dot_nt(a, b):
    return lax.dot_general(a, b, (((1,), (1,)), ((), ())), preferred_element_type=F32)


def _rms(x, g):
    ms = jnp.mean(x * x, axis=-1, keepdims=True)
    return x * lax.rsqrt(ms + EPS) * g


def _resident(shape):
    zeros = (0,) * len(shape)
    return pl.BlockSpec(shape, lambda *_: zeros, pipeline_mode=pl.Buffered(1))


def _params(*sem):
    return pltpu.CompilerParams(dimension_semantics=sem, vmem_limit_bytes=VMEM_LIMIT)


CONV_CHUNK = 512


def _mix0_kernel(x_ref, g_ref, win_ref, cw_ref, wout_ref, o_ref, carry_ref, *, tm):
    @pl.when(pl.program_id(1) == 0)
    def _():
        carry_ref[...] = jnp.zeros_like(carry_ref)

    x = x_ref[0]
    h = _rms(x, g_ref[...]).astype(BF16)
    row = lax.broadcasted_iota(jnp.int32, (tm, CONV_CHUNK), 0)
    acc = jnp.zeros((tm, D_MODEL), F32)
    for c in range(D_MODEL // CONV_CHUNK):
        lo, hi = c * CONV_CHUNK, (c + 1) * CONV_CHUNK
        b_gate = _dot(h, win_ref[:, lo:hi])
        c_gate = _dot(h, win_ref[:, D_MODEL + lo:D_MODEL + hi])
        u = _dot(h, win_ref[:, 2 * D_MODEL + lo:2 * D_MODEL + hi])
        v = c_gate * u
        prev = carry_ref[:, lo:hi]
        v1 = jnp.where(row == 0, prev[7:8], pltpu.roll(v, 1, axis=0))
        v2 = pltpu.roll(v, 2, axis=0)
        v2 = jnp.where(row == 0, prev[6:7], jnp.where(row == 1, prev[7:8], v2))
        carry_ref[:, lo:hi] = v[tm - 8:tm]
        conv = cw_ref[0:1, lo:hi] * v2 + cw_ref[1:2, lo:hi] * v1 + cw_ref[2:3, lo:hi] * v
        acc = acc + _dot((b_gate * conv).astype(BF16), wout_ref[lo:hi, :])
    o_ref[0] = x + acc


def _mix0(x, g, w_in, conv_w, w_out, tm):
    B, S, D = x.shape
    return pl.pallas_call(
        functools.partial(_mix0_kernel, tm=tm),
        out_shape=jax.ShapeDtypeStruct((B, S, D), F32),
        grid=(B, S // tm),
        in_specs=[
            pl.BlockSpec((1, tm, D), lambda b, j: (b, j, 0)),
            _resident((1, D)),
            _resident((D, 3 * D)),
            _resident((CONV_WIDTH, D)),
            _resident((D, D)),
        ],
        out_specs=pl.BlockSpec((1, tm, D), lambda b, j: (b, j, 0)),
        scratch_shapes=[pltpu.VMEM((8, D), F32)],
        compiler_params=_params("arbitrary", "arbitrary"),
        name="mix0",
    )(x, g, w_in, conv_w, w_out)


FF_CHUNK = 1024


def _mlp_kernel(*refs, has_proj, has_final):
    refs = list(refs)
    x_ref = refs.pop(0)
    if has_proj:
        a_ref, wo_ref = refs.pop(0), refs.pop(0)
    g_ref, w1_ref, w2_ref = refs.pop(0), refs.pop(0), refs.pop(0)
    if has_final:
        gf_ref = refs.pop(0)
    (o_ref,) = refs

    x = x_ref[...]
    if has_proj:
        x = x + _dot(a_ref[...], wo_ref[...])
    h = _rms(x, g_ref[...]).astype(BF16)
    acc = jnp.zeros_like(x)
    for c in range(D_FF // FF_CHUNK):
        lo, hi = c * FF_CHUNK, (c + 1) * FF_CHUNK
        a = jnp.maximum(_dot(h, w1_ref[:, lo:hi]), 0.0)
        acc = acc + _dot((a * a).astype(BF16), w2_ref[lo:hi, :])
    y = x + acc
    if has_final:
        y = _rms(y, gf_ref[...])
    o_ref[...] = y


def _mlp(x, g, w1, w2, tm, attn=None, w_o=None, g_final=None):
    T, D = x.shape
    has_proj, has_final = attn is not None, g_final is not None
    row = pl.BlockSpec((tm, D), lambda i: (i, 0))
    args, specs = [x], [row]
    if has_proj:
        args += [attn, w_o]
        specs += [row, _resident((D, D))]
    args += [g, w1, w2]
    specs += [_resident((1, D)), _resident((D, D_FF)), _resident((D_FF, D))]
    if has_final:
        args.append(g_final)
        specs.append(_resident((1, D)))
    return pl.pallas_call(
        functools.partial(_mlp_kernel, has_proj=has_proj, has_final=has_final),
        out_shape=jax.ShapeDtypeStruct((T, D), F32),
        grid=(T // tm,),
        in_specs=specs,
        out_specs=row,
        compiler_params=_params("arbitrary"),
        name="mlp_out" if has_proj else "mlp",
    )(*args)


TOK_COLS = 2 * N_GROUPS * HEAD_DIM + 2 * N_GROUPS * KA
FEAT_ROWS = 2 * N_GROUPS * VA
CHUNK_LANES = CMP_STRIDE * 2 * HEAD_DIM


def _kvproj_kernel(x_ref, g_ref, wtok_ref, wfeat_ref,
                   kraw_ref, vraw_ref, ksel_ref, kwin_ref, vsel_ref, vwin_ref, raw_ref, *, tm):
    gd = N_GROUPS * HEAD_DIM
    h = _rms(x_ref[0], g_ref[...]).astype(BF16)
    tok = _dot(h, wtok_ref[...])
    pair_lanes = 2 * HEAD_DIM
    for t in range(2 * gd // pair_lanes):
        raw_ref[t] = tok[:, t * pair_lanes:(t + 1) * pair_lanes]
    for p in range(CMP_STRIDE):
        rows = pl.ds(p, tm // CMP_STRIDE, stride=CMP_STRIDE)
        slot = slice(p * pair_lanes, (p + 1) * pair_lanes)
        for pair in range(N_GROUPS // 2):
            kraw_ref[0, pair, :, slot] = raw_ref[pair, rows, :].astype(BF16)
            vraw_ref[0, pair, :, slot] = raw_ref[N_GROUPS // 2 + pair, rows, :].astype(BF16)
    pos = pl.program_id(1) * tm + lax.broadcasted_iota(jnp.int32, (tm, NSEL_MAX), 0)
    lane = lax.broadcasted_iota(jnp.int32, (tm, NSEL_MAX), 1)
    onehot = (lane == jnp.right_shift(pos, int(math.log2(SEL_BLOCK)))).astype(BF16)
    for g in range(N_GROUPS):
        base = 2 * gd + g * KA
        ksel_ref[0, g, :, 0:KA] = tok[:, base:base + KA].astype(BF16)
        ksel_ref[0, g, :, KA:KSEL] = onehot
        base = 2 * gd + N_GROUPS * KA + g * KA
        kwin_ref[0, g] = tok[:, base:base + KA].astype(BF16)
    feat = _dot_nt(wfeat_ref[...], h)
    ones_row = (lax.broadcasted_iota(jnp.int32, (VA, tm), 0) == HEAD_DIM).astype(F32)
    for g in range(N_GROUPS):
        vsel_ref[0, g] = (feat[g * VA:(g + 1) * VA] + ones_row).astype(BF16)
        base = N_GROUPS * VA + g * VA
        vwin_ref[0, g] = (feat[base:base + VA] + ones_row).astype(BF16)


def _kvproj(x, g, w_tok, w_feat, tm):
    B, S, D = x.shape
    gd = N_GROUPS * HEAD_DIM
    chunk_rows = pl.BlockSpec((1, N_GROUPS // 2, tm // CMP_STRIDE, CHUNK_LANES), lambda b, j: (b, 0, j, 0))
    return pl.pallas_call(
        functools.partial(_kvproj_kernel, tm=tm),
        out_shape=[
            jax.ShapeDtypeStruct((B, N_GROUPS // 2, S // CMP_STRIDE, CHUNK_LANES), BF16),
            jax.ShapeDtypeStruct((B, N_GROUPS // 2, S // CMP_STRIDE, CHUNK_LANES), BF16),
            jax.ShapeDtypeStruct((B, N_GROUPS, S, KSEL), BF16),
            jax.ShapeDtypeStruct((B, N_GROUPS, S, KA), BF16),
            jax.ShapeDtypeStruct((B, N_GROUPS, VA, S), BF16),
            jax.ShapeDtypeStruct((B, N_GROUPS, VA, S), BF16),
        ],
        grid=(B, S // tm),
        in_specs=[
            pl.BlockSpec((1, tm, D), lambda b, j: (b, j, 0)),
            _resident((1, D)),
            _resident((D, TOK_COLS)),
            _resident((FEAT_ROWS, D)),
        ],
        out_specs=[
            chunk_rows, chunk_rows,
            pl.BlockSpec((1, N_GROUPS, tm, KSEL), lambda b, j: (b, 0, j, 0)),
            pl.BlockSpec((1, N_GROUPS, tm, KA), lambda b, j: (b, 0, j, 0)),
            pl.BlockSpec((1, N_GROUPS, VA, tm), lambda b, j: (b, 0, 0, j)),
            pl.BlockSpec((1, N_GROUPS, VA, tm), lambda b, j: (b, 0, 0, j)),
        ],
        scratch_shapes=[pltpu.VMEM((N_GROUPS, tm, 2 * HEAD_DIM), F32)],
        compiler_params=_params("arbitrary", "arbitrary"),
        name="kvproj",
    )(x, g, w_tok, w_feat)


def _compress_kernel(tk_ref, tv_ref, w1kx_ref, w1vx_ref, w1k_ref, w1v_ref, pek_ref, pev_ref, w2k_ref, w2vt_ref,
                     kc_ref, vct_ref, *, nc):
    half = CMP_STRIDE * HEAD_DIM

    def hidden(t_ref, w1x_ref, w1_ref, pe_ref):
        ab = _dot(t_ref[0, 0], w1x_ref[0])
        pe = (_dot(pe_ref[:, 0:half], w1_ref[:, 0:PHI_HIDDEN])
              + _dot(pe_ref[:, half:2 * half], w1_ref[:, PHI_HIDDEN:2 * PHI_HIDDEN]))
        nxt = pltpu.roll(ab[:, PHI_HIDDEN:2 * PHI_HIDDEN], nc - 1, axis=0)
        z = ab[:, 0:PHI_HIDDEN] + nxt + pe[0:1]
        return (z * jax.nn.sigmoid(z)).astype(BF16)

    kc = _dot(hidden(tk_ref, w1kx_ref, w1k_ref, pek_ref), w2k_ref[...])
    row = lax.broadcasted_iota(jnp.int32, (nc, KA), 0)
    lane = lax.broadcasted_iota(jnp.int32, (nc, KA), 1)
    group = (lane == HEAD_DIM + jnp.right_shift(row, 3)).astype(F32)
    kc_ref[0, 0] = (jnp.where(row < nc - 1, kc, 0.0) + group).astype(BF16)
    vct = _dot_nt(w2vt_ref[...], hidden(tv_ref, w1vx_ref, w1v_ref, pev_ref))
    real = lax.broadcasted_iota(jnp.int32, (VA, nc), 1) < nc - 1
    ones_row = (lax.broadcasted_iota(jnp.int32, (VA, nc), 0) == HEAD_DIM).astype(F32)
    vct_ref[0, 0] = (jnp.where(real, vct, 0.0) + ones_row).astype(BF16)


def _compress(tk, tv, w1kx, w1vx, w1k, w1v, pek, pev, w2k, w2vt):
    B, _, nc, width = tk.shape
    G = N_GROUPS
    blk = pl.BlockSpec((1, 1, nc, width), lambda b, g: (b, g // 2, 0, 0))
    spread = pl.BlockSpec((1,) + w1kx.shape[1:], lambda b, g: (g % 2, 0, 0))
    return pl.pallas_call(
        functools.partial(_compress_kernel, nc=nc),
        out_shape=[
            jax.ShapeDtypeStruct((B, G, nc, KA), BF16),
            jax.ShapeDtypeStruct((B, G, VA, nc), BF16),
        ],
        grid=(B, G),
        in_specs=[
            blk, blk, spread, spread,
            _resident(w1k.shape), _resident(w1v.shape),
            _resident(pek.shape), _resident(pev.shape),
            _resident(w2k.shape), _resident(w2vt.shape),
        ],
        out_specs=[
            pl.BlockSpec((1, 1, nc, KA), lambda b, g: (b, g, 0, 0)),
            pl.BlockSpec((1, 1, VA, nc), lambda b, g: (b, g, 0, 0)),
        ],
        compiler_params=_params("arbitrary", "arbitrary"),
        name="compress",
    )(tk, tv, w1kx, w1vx, w1k, w1v, pek, pev, w2k, w2vt)


GATE_ROWS = 16


def _qproj_kernel(x_ref, g_ref, wq_ref, wg_ref, q_ref, gate_ref):
    h = _rms(x_ref[0], g_ref[...]).astype(BF16)
    q_ref[0] = (_dot_nt(wq_ref[...], h) * (HEAD_DIM ** -0.5 * LOG2E)).astype(BF16)
    gate_ref[0] = jax.nn.sigmoid(_dot_nt(wg_ref[...], h))


def _qproj(x, g, wq_t, wg_t, tm):
    B, S, D = x.shape
    hd = N_HEADS * HEAD_DIM
    gr = N_GROUPS * GATE_ROWS
    return pl.pallas_call(
        _qproj_kernel,
        out_shape=[
            jax.ShapeDtypeStruct((B, hd, S), BF16),
            jax.ShapeDtypeStruct((B, gr, S), F32),
        ],
        grid=(B, S // tm),
        in_specs=[
            pl.BlockSpec((1, tm, D), lambda b, j: (b, j, 0)),
            _resident((1, D)),
            _resident((hd, D)),
            _resident((gr, D)),
        ],
        out_specs=[
            pl.BlockSpec((1, hd, tm), lambda b, j: (b, 0, j)),
            pl.BlockSpec((1, gr, tm), lambda b, j: (b, 0, j)),
        ],
        compiler_params=_params("arbitrary", "arbitrary"),
        name="qproj",
    )(x, g, wq_t, wg_t)


def _t5_bucket(dist):
    max_exact = N_BUCKETS // 2
    logd = jnp.log(jnp.maximum(dist, 1).astype(F32) / max_exact)
    large = max_exact + (logd / math.log(MAX_DISTANCE / max_exact)
                         * (N_BUCKETS - max_exact)).astype(jnp.int32)
    large = jnp.minimum(large, N_BUCKETS - 1)
    return jnp.where(dist < 0, -1, jnp.where(dist < max_exact, dist, large))


def _bucket_tables():
    qry = jnp.arange(QT, dtype=jnp.int32)[None, :]
    dist = qry + WINDOW - jnp.arange(TAB_ROWS, dtype=jnp.int32)[:, None]
    win = jnp.where(dist < WINDOW, _t5_bucket(dist), -1)
    blk = jnp.arange(CMP_TAB, dtype=jnp.int32)[:, None] - (CMP_TAB - QT // CMP_STRIDE)
    cmp_ = _t5_bucket(qry - CMP_STRIDE * blk - (CMP_BLOCK - 1))
    return win, cmp_


def _bias_kernel(rb_ref, win_ref, cmp_ref, twin_ref, tcmp_ref):
    g = pl.program_id(0)

    def table(bucket_ref, out_ref):
        bucket = bucket_ref[...]
        for hg in range(HPG):
            head = g * HPG + hg
            t = jnp.zeros(bucket.shape, F32)
            for b in range(N_BUCKETS):
                t = jnp.where(bucket == b, rb_ref[b, head], t)
            t = (t - rb_ref[N_BUCKETS - 1, head]) * LOG2E
            out_ref[0, :, hg * QT:(hg + 1) * QT] = jnp.where(bucket >= 0, t, NEG_INF)

    table(win_ref, twin_ref)
    table(cmp_ref, tcmp_ref)


def _bias_tables(rel_bias):
    win, cmp_ = _bucket_tables()
    whole = lambda a: pl.BlockSpec(a.shape, lambda g: (0, 0))
    return pl.pallas_call(
        _bias_kernel,
        out_shape=[
            jax.ShapeDtypeStruct((N_GROUPS, TAB_ROWS, QL), F32),
            jax.ShapeDtypeStruct((N_GROUPS, CMP_TAB, QL), F32),
        ],
        grid=(N_GROUPS,),
        in_specs=[pl.BlockSpec(memory_space=pltpu.SMEM), whole(win), whole(cmp_)],
        out_specs=[
            pl.BlockSpec((1, TAB_ROWS, QL), lambda g: (g, 0, 0)),
            pl.BlockSpec((1, CMP_TAB, QL), lambda g: (g, 0, 0)),
        ],
        compiler_params=_params("arbitrary"),
        name="bias_tables",
    )(rel_bias, win, cmp_)


SC_PAD = 16
IMP_PAD = 8
M_INIT = -1e29
EXP2_HEADROOM = 100.0


def _attn_kernel(q_ref, gate_ref, kc_ref, vct_ref, ks_ref, vst_ref, kw_ref, vwt_ref, twin_ref, tcmp_ref, o_ref,
                 wn_ref, wf_ref, sc_ref, imp_ref, accs_ref, ms_ref, ex_ref, *, nc):
    qt = pl.program_id(2)
    t0 = pl.multiple_of(qt * QT, QT)
    nsel = nc // (SEL_BLOCK // CMP_STRIDE)
    lanes = lambda hg: slice(hg * QT, (hg + 1) * QT)

    for hg in range(HPG):
        wn_ref[0:HEAD_DIM, lanes(hg)] = q_ref[0, hg * HEAD_DIM:(hg + 1) * HEAD_DIM, pl.ds(t0, QT)]
    group = lax.broadcasted_iota(jnp.int32, (KA - HEAD_DIM, QL), 0)
    wn_ref[HEAD_DIM:KA, :] = jnp.where(group <= qt, 0.0, NEG_INF).astype(BF16)

    def table(first_key, n):
        row0 = pl.multiple_of(first_key - (t0 - WINDOW), QT)
        return twin_ref[0, pl.ds(row0, n), :]

    sc_ref[0:SC_PAD, :] = jnp.zeros((SC_PAD, QL), F32)
    sc_ref[SC_PAD:SC_PAD + nc, :] = _dot(kc_ref[0, 0], wn_ref[0:KA, :])
    wstart = pl.multiple_of(jnp.maximum(t0 - WINDOW, 0), QT)
    s_win = _dot(kw_ref[0, 0, pl.ds(wstart, WIN_KEYS), :], wn_ref[0:KA, :]) + table(wstart, WIN_KEYS)
    win = pl.ds(pl.multiple_of(qt * (QT // CMP_STRIDE), 8), CMP_TAB)
    sc_ref[win, :] = sc_ref[win, :] + tcmp_ref[0]
    s = sc_ref[SC_PAD:SC_PAD + nc, :]
    m = jnp.max(s, axis=0, keepdims=True)
    e = jnp.exp2(s - m)
    inv_sum = jnp.where(m > 0.5 * NEG_INF, 1.0 / jnp.sum(e, axis=0, keepdims=True), 0.0)
    o_cmp = _dot(vct_ref[0, 0], e.astype(BF16)) * inv_sum

    p_win = jnp.exp2(s_win - jnp.max(s_win, axis=0, keepdims=True)).astype(BF16)
    acc_win = _dot(vwt_ref[0, 0, :, pl.ds(wstart, WIN_KEYS)], p_win)

    imp_ref[0:IMP_PAD, :] = jnp.zeros((IMP_PAD, QT), F32)
    imp_ref[IMP_PAD:IMP_PAD + nc, :] = (e[:, lanes(0)] * inv_sum[:, lanes(0)] + e[:, lanes(1)] * inv_sum[:, lanes(1)]
                                        + e[:, lanes(2)] * inv_sum[:, lanes(2)] + e[:, lanes(3)] * inv_sum[:, lanes(3)])
    imp_ref[IMP_PAD + nc:IMP_PAD + nc + 8, :] = jnp.zeros((8, QT), F32)
    ratio = SEL_BLOCK // CMP_STRIDE
    part = lambda r: imp_ref[pl.ds(IMP_PAD + r, nsel, stride=ratio), :]
    imp_s = part(-1) + 2.0 * (part(0) + part(1) + part(2)) + part(3)

    jb = lax.broadcasted_iota(jnp.int32, (nsel, QT), 0)
    cb = 2 * qt + (lax.broadcasted_iota(jnp.int32, (nsel, QT), 1) >= SEL_BLOCK).astype(jnp.int32)
    forced = (jb == 0) | (jb == cb) | (jb == cb - 1)
    candidate = (jb <= cb) & jnp.logical_not(forced)
    score = jnp.where(candidate, imp_s, NEG_INF)
    jbf = jb.astype(F32)

    def take_one(_, score):
        top = jnp.max(score, axis=0, keepdims=True)
        first = jnp.min(jnp.where(score == top, jbf, float(nsel)), axis=0, keepdims=True)
        return jnp.where(jbf == first, REMOVED_SCORE, score)

    score = lax.fori_loop(0, SEL_TOP_N - 3, take_one, score, unroll=True)
    sel = (forced & (jb <= cb)) | (candidate & (score < 2.0 * NEG_INF))
    mask = jnp.where(sel, 0.0, NEG_INF)
    mask_far = jnp.where(jb >= 2 * qt - 2, NEG_INF, mask).astype(BF16)
    mask = mask.astype(BF16)
    for hg in range(HPG):
        wn_ref[KA:KA + nsel, lanes(hg)] = mask
        wf_ref[KA:KA + nsel, lanes(hg)] = mask_far
    if nsel < NSEL_MAX:
        wn_ref[KA + nsel:KSEL, :] = jnp.zeros((NSEL_MAX - nsel, QL), BF16)
        wf_ref[KA + nsel:KSEL, :] = jnp.zeros((NSEL_MAX - nsel, QL), BF16)
    wf_ref[0:KA, :] = wn_ref[0:KA, :]

    def accumulate(acc_ref, m_ref, s, s_max, vt):
        m_old = m_ref[0:1, :]
        m_new = jnp.maximum(m_old, s_max)
        p = jnp.exp2(s - m_new).astype(BF16)
        acc_ref[...] = jnp.exp2(m_old - m_new) * acc_ref[...] + _dot(vt, p)
        m_ref[0:1, :] = m_new

    def reset(acc_ref, m_ref):
        acc_ref[...] = jnp.zeros_like(acc_ref)
        m_ref[...] = jnp.full(m_ref.shape, M_INIT, F32)

    near = pl.multiple_of(jnp.maximum(t0 - QT, 0), QT)
    far_tiles = (qt + FAR_KEYS // QT - 2) // (FAR_KEYS // QT)

    def far_keys(tile):
        start = pl.multiple_of(tile * FAR_KEYS, FAR_KEYS)
        return ks_ref[0, 0, pl.ds(start, FAR_KEYS), :], vst_ref[0, 0, :, pl.ds(start, FAR_KEYS)]

    reset(accs_ref, ms_ref)
    s_near = _dot(ks_ref[0, 0, pl.ds(near, 2 * QT), :], wn_ref[...]) + table(near, 2 * QT)
    accumulate(accs_ref, ms_ref, s_near, jnp.max(s_near, axis=0, keepdims=True),
               vst_ref[0, 0, :, pl.ds(near, 2 * QT)])
    ref_level = ms_ref[0:1, :]

    def far_fast(tiles):
        acc, excess = accs_ref[...], ex_ref[0:1, :]
        for tile in tiles:
            k, vt = far_keys(tile)
            s = _dot(k, wf_ref[...]) - ref_level
            acc = acc + _dot(vt, jnp.exp2(s).astype(BF16))
            excess = jnp.maximum(excess, jnp.max(s, axis=0, keepdims=True))
        accs_ref[...] = acc
        ex_ref[0:1, :] = excess

    ex_ref[...] = jnp.full(ex_ref.shape, NEG_INF, F32)

    def far_four(i, _):
        far_fast((4 * i, 4 * i + 1, 4 * i + 2, 4 * i + 3))
        return 0

    lax.fori_loop(0, far_tiles // 4, far_four, 0)
    done = far_tiles // 4 * 4

    @pl.when(far_tiles % 4 >= 2)
    def _():
        far_fast((done, done + 1))

    @pl.when(far_tiles % 2 == 1)
    def _():
        far_fast((far_tiles - 1,))

    @pl.when(jnp.max(ex_ref[0:1, :]) > EXP2_HEADROOM)
    def _():
        reset(accs_ref, ms_ref)
        s_again = _dot(ks_ref[0, 0, pl.ds(near, 2 * QT), :], wn_ref[...]) + table(near, 2 * QT)
        accumulate(accs_ref, ms_ref, s_again, jnp.max(s_again, axis=0, keepdims=True),
                   vst_ref[0, 0, :, pl.ds(near, 2 * QT)])

        def far_tile_exact(tile, _):
            k, vt = far_keys(tile)
            s = _dot(k, wf_ref[...])
            accumulate(accs_ref, ms_ref, s, jnp.max(s, axis=0, keepdims=True), vt)
            return 0

        lax.fori_loop(0, far_tiles, far_tile_exact, 0)

    acc_sel = accs_ref[...]

    o_sel = acc_sel[0:HEAD_DIM, :] * (1.0 / acc_sel[HEAD_DIM:HEAD_DIM + 1, :])
    o_win = acc_win[0:HEAD_DIM, :] * (1.0 / acc_win[HEAD_DIM:HEAD_DIM + 1, :])
    heads = []
    for hg in range(HPG):
        gate = lambda br: gate_ref[0, hg * 3 + br:hg * 3 + br + 1, pl.ds(t0, QT)]
        heads.append(gate(0) * o_cmp[0:HEAD_DIM, lanes(hg)] + gate(1) * o_sel[:, lanes(hg)]
                     + gate(2) * o_win[:, lanes(hg)])
    for pair in range(HPG // 2):
        both = jnp.concatenate(heads[2 * pair:2 * pair + 2], axis=0)
        o_ref[0, pl.ds(t0, QT), pair * 2 * HEAD_DIM:(pair + 1) * 2 * HEAD_DIM] = both.T.astype(BF16)


def _attention(q_t, gates_t, kc, vct, ks, vst, kw, vwt, twin, tcmp):
    B, hd, S = q_t.shape
    nc = kc.shape[2]
    per_bg = lambda *blk: pl.BlockSpec((1, 1) + blk, lambda b, g, j: (b, g, 0, 0))
    per_g = lambda rows: pl.BlockSpec((1, rows, QL), lambda b, g, j: (g, 0, 0))
    return pl.pallas_call(
        functools.partial(_attn_kernel, nc=nc),
        out_shape=jax.ShapeDtypeStruct((B, S, hd), BF16),
        grid=(B, N_GROUPS, S // QT),
        in_specs=[
            pl.BlockSpec((1, HPG * HEAD_DIM, S), lambda b, g, j: (b, g, 0)),
            pl.BlockSpec((1, GATE_ROWS, S), lambda b, g, j: (b, g, 0)),
            per_bg(nc, KA), per_bg(VA, nc),
            per_bg(S, KSEL), per_bg(VA, S),
            per_bg(S, KA), per_bg(VA, S),
            per_g(TAB_ROWS), per_g(CMP_TAB),
        ],
        out_specs=pl.BlockSpec((1, S, HPG * HEAD_DIM), lambda b, g, j: (b, 0, g)),
        scratch_shapes=[
            pltpu.VMEM((KSEL, QL), BF16),
            pltpu.VMEM((KSEL, QL), BF16),
            pltpu.VMEM((SC_PAD + nc, QL), F32),
            pltpu.VMEM((IMP_PAD + nc + 8, QT), F32),
            pltpu.VMEM((VA, QL), F32), pltpu.VMEM((8, QL), F32),
            pltpu.VMEM((8, QL), F32),
        ],
        compiler_params=_params("arbitrary", "arbitrary", "arbitrary"),
        name="nsa_attention",
    )(q_t, gates_t, kc, vct, ks, vst, kw, vwt, twin, tcmp)


def _pad_groups(w, width):
    d = w.shape[0]
    w = w.reshape(d, N_GROUPS, HEAD_DIM)
    return jnp.pad(w, ((0, 0), (0, 0), (0, width - HEAD_DIM))).reshape(d, N_GROUPS * width)


def kernel(x, rel_bias, norm_mix, norm_mlp, mlp_w1, mlp_w2, a_w_in, a_conv_w, a_w_out, kv_norm, w_kv,
           cmp_pe_k, cmp_pe_v, phi_k_w1, phi_k_w2, phi_v_w1, phi_v_w2, b_w_qg, b_w_o, final_norm):
    B, S, D = x.shape
    assert D == D_MODEL and S % (2 * FAR_KEYS) == 0 and S <= NSEL_MAX * SEL_BLOCK and norm_mix.shape[0] == 2
    tm = min(512, S)
    nc = S // CMP_STRIDE
    gd = N_GROUPS * HEAD_DIM
    row = lambda v: v.reshape(1, D).astype(F32)

    x = _mix0(x, row(norm_mix[0]), a_w_in[0].astype(BF16), a_conv_w[0], a_w_out[0].astype(BF16), tm)
    x = _mlp(x.reshape(B * S, D), row(norm_mlp[0]), mlp_w1[0].astype(BF16), mlp_w2[0].astype(BF16), tm)
    x = x.reshape(B, S, D)

    part = lambda c: w_kv[:, c * gd:(c + 1) * gd]
    w_tok = jnp.concatenate([part(0), part(1), _pad_groups(part(2), KA), _pad_groups(part(4), KA)], axis=1)
    w_feat = jnp.concatenate([_pad_groups(part(3), VA).T, _pad_groups(part(5), VA).T], axis=0)
    k_raw, v_raw, k_sel, k_win, v_sel_t, v_win_t = _kvproj(
        x, row(kv_norm), w_tok.astype(BF16), w_feat.astype(BF16), tm)

    def spread(w):
        w = w.reshape(CMP_STRIDE, HEAD_DIM, 2 * PHI_HIDDEN)
        z = jnp.zeros_like(w)
        return jnp.stack([jnp.concatenate([w, z], axis=1), jnp.concatenate([z, w], axis=1)]
                         ).reshape(2, CHUNK_LANES, 2 * PHI_HIDDEN)

    half = CMP_STRIDE * HEAD_DIM
    w1cat = lambda w: jnp.concatenate([w[:half], w[half:]], axis=1).astype(BF16)
    pe_rows = lambda pe: jnp.broadcast_to(pe.reshape(1, 2 * half), (16, 2 * half)).astype(BF16)
    w2k = jnp.pad(phi_k_w2, ((0, 0), (0, KA - HEAD_DIM))).astype(BF16)
    w2vt = jnp.pad(phi_v_w2.T, ((0, VA - HEAD_DIM), (0, 0))).astype(BF16)
    k_cmp, v_cmp_t = _compress(k_raw, v_raw, spread(w1cat(phi_k_w1)), spread(w1cat(phi_v_w1)),
                               w1cat(phi_k_w1), w1cat(phi_v_w1),
                               pe_rows(cmp_pe_k), pe_rows(cmp_pe_v), w2k, w2vt)

    w_qg = b_w_qg[0]
    wq_t = w_qg[:, :N_HEADS * HEAD_DIM].T.astype(BF16)
    wg = w_qg[:, N_HEADS * HEAD_DIM:].reshape(D, N_GROUPS, HPG * 3)
    wg_t = jnp.pad(wg, ((0, 0), (0, 0), (0, GATE_ROWS - HPG * 3))).reshape(D, N_GROUPS * GATE_ROWS).T
    q_t, gates_t = _qproj(x, row(norm_mix[1]), wq_t, wg_t.astype(BF16), tm)

    twin, tcmp = _bias_tables(rel_bias)
    attn = _attention(q_t, gates_t, k_cmp, v_cmp_t, k_sel, v_sel_t, k_win, v_win_t, twin, tcmp)

    out = _mlp(x.reshape(B * S, D), row(norm_mlp[1]), mlp_w1[1].astype(BF16), mlp_w2[1].astype(BF16), tm,
               attn=attn.reshape(B * S, D), w_o=b_w_o[0].astype(BF16), g_final=row(final_norm))
    return out.reshape(B, S, D)
```
